```python
import jax
import jax.numpy as jnp
from jax import lax
import numpy as np

D_MODEL = 2048
BATCH = 8
SEQ = 2048
DEPTH = 1
DEC_BATCH = 32
DEC_SEQ = 8
PAST_LEN = 16384
PAGE_SIZE = 128

C_CONV = 1024
CONV_W = 31
SW_GROUPS = 3
SW_WINDOWS = (128, 512, 2048)
SW_DILATIONS = (1, 4, 16)
SW_SPAN = 128
SW_HG = 4
SW_HD = 128
SW_OUT = SW_HG * SW_HD
SW_BLK = 128
N_MEM = 256
MEM_H = 4
MEM_HD = 256
MEM_W = MEM_H * MEM_HD
N_BRANCH = 3
IN_CONV = 2 * C_CONV
IN_SW = SW_GROUPS * 3 * SW_OUT
IN_MEMQ = MEM_W
IN_GATE = N_BRANCH * D_MODEL
IN_COLS = IN_CONV + IN_SW + IN_MEMQ + IN_GATE
PEER_HEADS = 8
PEER_NKEYS = 128
PEER_N = PEER_NKEYS * PEER_NKEYS
PEER_DKEY = 256
PEER_TOPK = 16
PEER_BLK = 128
EPS = 1e-6

kernel_name = 'hybrid_conv_dilated_mem_peer_step'


def rmsnorm(x, g):
    xf = x.astype(jnp.float32)
    y = xf * lax.rsqrt(jnp.mean(xf * xf, axis=-1, keepdims=True) + EPS)
    return (y * g.astype(jnp.float32)).astype(x.dtype)


def layernorm(x, g, b):
    xf = x.astype(jnp.float32)
    mu = jnp.mean(xf, axis=-1, keepdims=True)
    xc = xf - mu
    var = jnp.mean(xc * xc, axis=-1, keepdims=True)
    return (xc * lax.rsqrt(var + EPS) * g.astype(jnp.float32) + b.astype(jnp.float32)).astype(x.dtype)


def alibi_slopes():
    n = SW_GROUPS * SW_HG
    s = 2.0 ** (-8.0 * jnp.arange(1, n + 1, dtype=jnp.float32) / n)
    return s.reshape(SW_GROUPS, SW_HG)


def conv_branch(h_glu, prev, conv_w, conv_b, ln_g, ln_b, w_o):
    a, b = jnp.split(h_glu, 2, axis=-1)
    u = a * jax.nn.sigmoid(b)
    u_ext = jnp.concatenate([prev, u], axis=1)
    c = lax.conv_general_dilated(u_ext, conv_w[:, None, :].astype(u_ext.dtype), window_strides=(1,),
                                 padding='VALID', dimension_numbers=('NWC', 'WIO', 'NWC'),
                                 feature_group_count=C_CONV) + conv_b
    c = jax.nn.silu(layernorm(c, ln_g, ln_b))
    return c @ w_o, u_ext[:, -(CONV_W - 1):]


def dilated_band_attend(q, k, v, r, slopes):
    B, S, H, D = q.shape
    U = S // r
    nb = -(-U // SW_BLK)
    Up = nb * SW_BLK

    def classes(a):
        a = a.reshape(B, U, r, H, D).transpose(0, 2, 1, 3, 4)
        a = jnp.pad(a, ((0, 0), (0, 0), (0, Up - U), (0, 0), (0, 0)))
        return a.reshape(B, r, nb, SW_BLK, H, D)

    def band(a):
        prev = jnp.pad(a, ((0, 0), (0, 0), (1, 0), (0, 0), (0, 0), (0, 0)))[:, :, :nb]
        return jnp.concatenate([prev, a], axis=3)

    qb = classes(q)
    kw = band(classes(k))
    vw = band(classes(v)).astype(jnp.float32)
    s = jnp.einsum('brnihd,brnchd->brnhic', qb, kw).astype(jnp.float32)
    i = jnp.arange(SW_BLK)[:, None]
    c = jnp.arange(2 * SW_BLK)[None, :]
    j = SW_BLK + i - c
    u_key = (jnp.arange(nb)[:, None, None] - 1) * SW_BLK + c[None]
    valid = (j >= 0) & (j <= SW_SPAN) & (u_key >= 0)
    bias = -slopes[:, None, None] * (r * j).astype(jnp.float32)[None]
    s = jnp.where(valid[None, None, :, None], s + bias[None, None, None], -jnp.inf)
    m = jnp.max(s, axis=-1, keepdims=True)
    p = jnp.exp(s - m)
    l = jnp.sum(p, axis=-1, keepdims=True)
    o = jnp.einsum('brnhic,brnchd->brnhid', p, vw) / l
    lse = (m + jnp.log(l))[..., 0]
    o = o.transpose(0, 1, 2, 4, 3, 5).reshape(B, r, Up, H, D)[:, :, :U]
    o = o.transpose(0, 2, 1, 3, 4).reshape(B, S, H, D)
    lse = lse.transpose(0, 1, 2, 4, 3).reshape(B, r, Up, H)[:, :, :U]
    lse = lse.transpose(0, 2, 1, 3).reshape(B, S, H)
    return o, lse


def dilated_gather_attend(q, k_new, v_new, k_buf, v_buf, r, slopes):
    B, T, H, D = q.shape
    L = k_buf.shape[1]
    k_all = jnp.concatenate([k_buf, k_new], axis=1)
    v_all = jnp.concatenate([v_buf, v_new], axis=1)
    jj = jnp.arange(SW_SPAN + 1)
    idx = L + jnp.arange(T)[:, None] - r * jj[None, :]
    valid = idx >= 0
    idxc = jnp.maximum(idx, 0)
    kg = k_all[:, idxc]
    vg = v_all[:, idxc].astype(jnp.float32)
    s = jnp.einsum('bthd,btjhd->bhtj', q, kg).astype(jnp.float32)
    bias = -slopes[:, None, None] * (r * jj).astype(jnp.float32)[None, None, :]
    s = jnp.where(valid[None, None], s + bias[None], -jnp.inf)
    m = jnp.max(s, axis=-1, keepdims=True)
    p = jnp.exp(s - m)
    l = jnp.sum(p, axis=-1, keepdims=True)
    o = (jnp.einsum('bhtj,btjhd->bhtd', p, vg) / l).transpose(0, 2, 1, 3)
    lse = (m + jnp.log(l))[..., 0].transpose(0, 2, 1)
    return o, lse, k_all[:, T:], v_all[:, T:]


def combine_groups(outs, lses):
    wts = jax.nn.softmax(jnp.stack(lses, axis=0), axis=0)
    return jnp.einsum('gbth,gbthd->bthd', wts, jnp.stack(outs, axis=0))


def sw_prompt(q, k, v):
    T = q.shape[1]
    slopes = alibi_slopes()
    outs, lses, states = [], [], []
    for g in range(SW_GROUPS):
        o, lse = dilated_band_attend(q[:, :, g], k[:, :, g], v[:, :, g], SW_DILATIONS[g], slopes[g])
        outs.append(o)
        lses.append(lse)
        keep = min(SW_WINDOWS[g], T)
        states += [k[:, T - keep:, g], v[:, T - keep:, g]]
    return combine_groups(outs, lses), states


def sw_sample(q, k, v, bufs):
    slopes = alibi_slopes()
    outs, lses, states = [], [], []
    for g in range(SW_GROUPS):
        o, lse, kb, vb = dilated_gather_attend(q[:, :, g], k[:, :, g], v[:, :, g], bufs[2 * g], bufs[2 * g + 1],
                                               SW_DILATIONS[g], slopes[g])
        outs.append(o)
        lses.append(lse)
        states += [kb, vb]
    return combine_groups(outs, lses), states


def memory_kv(mem, mem_norm_g, w_k, w_v, k_g):
    B = mem.shape[0]
    mn = rmsnorm(mem, mem_norm_g)
    k = rmsnorm((mn @ w_k).reshape(B, N_MEM, MEM_H, MEM_HD), k_g)
    v = (mn @ w_v).reshape(B, N_MEM, MEM_H, MEM_HD)
    return k, v


def memory_attend(q, k, v):
    s = jnp.einsum('bthd,bshd->bhts', q, k).astype(jnp.float32)
    p = jax.nn.softmax(s, axis=-1).astype(v.dtype)
    return jnp.einsum('bhts,bshd->bthd', p, v)


def peer_ffn(xn, w_q, k1, k2, wu, wv):
    B, T, Dm = xn.shape
    n = B * T
    nb = -(-n // PEER_BLK)
    xt = jnp.pad(xn.reshape(n, Dm), ((0, nb * PEER_BLK - n), (0, 0))).reshape(nb, PEER_BLK, Dm)

    def block(xb):
        q = (xb @ w_q).reshape(PEER_BLK, PEER_HEADS, 2, PEER_DKEY // 2)
        s1 = jnp.einsum('thd,kd->thk', q[:, :, 0], k1).astype(jnp.float32)
        s2 = jnp.einsum('thd,kd->thk', q[:, :, 1], k2).astype(jnp.float32)
        v1, i1 = lax.top_k(s1, PEER_TOPK)
        v2, i2 = lax.top_k(s2, PEER_TOPK)
        cand = (v1[..., :, None] + v2[..., None, :]).reshape(PEER_BLK, PEER_HEADS, PEER_TOPK * PEER_TOPK)
        cidx = (i1[..., :, None] * PEER_NKEYS + i2[..., None, :]).reshape(PEER_BLK, PEER_HEADS, PEER_TOPK * PEER_TOPK)
        sv, si = lax.top_k(cand, PEER_TOPK)
        eidx = jnp.take_along_axis(cidx, si, axis=-1)
        gate = jax.nn.softmax(sv, axis=-1)
        act = jax.nn.gelu(jnp.einsum('td,thkd->thk', xb, wu[eidx]).astype(jnp.float32))
        coef = (gate * act).astype(xb.dtype)
        return jnp.einsum('thk,thkd->td', coef, wv[eidx])

    out = lax.map(block, xt).reshape(nb * PEER_BLK, Dm)[:n]
    return out.reshape(B, T, Dm)


def layer(x, conv_prev, sw_bufs, mem_k, mem_v, norm1_g, w_in, conv_w, conv_b, conv_ln_g, conv_ln_b, w_conv_o,
          sw_q_g, sw_k_g, w_sw_o, mem_q_g, w_mem_o, w_out, norm2_g, peer_w_q, peer_k1, peer_k2, peer_u, peer_v):
    B, T, _ = x.shape
    xn = rmsnorm(x, norm1_g)
    hcat = xn @ w_in
    h_conv, h_sw, h_mq, h_gate = jnp.split(hcat, [IN_CONV, IN_CONV + IN_SW, IN_CONV + IN_SW + IN_MEMQ], axis=-1)
    conv_out, conv_state = conv_branch(h_conv, conv_prev, conv_w, conv_b, conv_ln_g, conv_ln_b, w_conv_o)
    qkv = h_sw.reshape(B, T, SW_GROUPS, 3, SW_HG, SW_HD)
    q = rmsnorm(qkv[:, :, :, 0], sw_q_g[:, None, :]) * (SW_HD ** -0.5)
    k = rmsnorm(qkv[:, :, :, 1], sw_k_g[:, None, :])
    v = qkv[:, :, :, 2]
    if sw_bufs is None:
        sw_o, sw_states = sw_prompt(q, k, v)
    else:
        sw_o, sw_states = sw_sample(q, k, v, sw_bufs)
    sw_out = sw_o.astype(x.dtype).reshape(B, T, SW_OUT) @ w_sw_o
    qm = rmsnorm(h_mq.reshape(B, T, MEM_H, MEM_HD), mem_q_g) * (MEM_HD ** -0.5)
    mem_out = memory_attend(qm, mem_k, mem_v).reshape(B, T, MEM_W) @ w_mem_o
    gates = jax.nn.sigmoid(h_gate.reshape(B, T, N_BRANCH, D_MODEL))
    merged = gates[:, :, 0] * conv_out + gates[:, :, 1] * sw_out + gates[:, :, 2] * mem_out
    h = x + merged @ w_out
    y = h + peer_ffn(rmsnorm(h, norm2_g), peer_w_q, peer_k1, peer_k2, peer_u, peer_v)
    return y, conv_state, sw_states


def setup_inputs(seed: int = 0) -> dict:
    key = jax.random.key(seed)
    ks = iter(jax.random.split(key, 48))

    def nrm(shape, scale):
        return jax.random.normal(next(ks), shape, jnp.float32) * scale

    def gain(shape):
        return 1.0 + nrm(shape, 0.02)

    inp = {}
    inp['x_prompt'] = nrm((BATCH, SEQ, D_MODEL), 1.0)
    inp['x_sample'] = nrm((DEC_BATCH, DEC_SEQ, D_MODEL), 1.0)
    inp['mem_prompt'] = nrm((BATCH, N_MEM, D_MODEL), 1.0)
    inp['state_conv'] = nrm((DEPTH, DEC_BATCH, CONV_W - 1, C_CONV), 1.0)
    for g in range(SW_GROUPS):
        L = min(SW_WINDOWS[g], PAST_LEN)
        inp['state_sw%d_k' % g] = nrm((DEPTH, DEC_BATCH, L, SW_HG, SW_HD), 1.0)
        inp['state_sw%d_v' % g] = nrm((DEPTH, DEC_BATCH, L, SW_HG, SW_HD), 1.0)
    inp['cache_mem_k'] = nrm((DEPTH, DEC_BATCH, N_MEM, MEM_H, MEM_HD), 1.0)
    inp['cache_mem_v'] = nrm((DEPTH, DEC_BATCH, N_MEM, MEM_H, MEM_HD), 1.0)
    inp['norm1_g'] = gain((DEPTH, D_MODEL))
    inp['mem_norm_g'] = gain((DEPTH, D_MODEL))
    inp['w_in'] = nrm((DEPTH, D_MODEL, IN_COLS), D_MODEL ** -0.5)
    inp['conv_w'] = nrm((DEPTH, CONV_W, C_CONV), CONV_W ** -0.5)
    inp['conv_b'] = nrm((DEPTH, C_CONV), 0.01)
    inp['conv_ln_g'] = gain((DEPTH, C_CONV))
    inp['conv_ln_b'] = nrm((DEPTH, C_CONV), 0.01)
    inp['w_conv_o'] = nrm((DEPTH, C_CONV, D_MODEL), C_CONV ** -0.5)
    inp['sw_q_g'] = gain((DEPTH, SW_GROUPS, SW_HD))
    inp['sw_k_g'] = gain((DEPTH, SW_GROUPS, SW_HD))
    inp['w_sw_o'] = nrm((DEPTH, SW_OUT, D_MODEL), SW_OUT ** -0.5)
    inp['w_mem_k'] = nrm((DEPTH, D_MODEL, MEM_W), D_MODEL ** -0.5)
    inp['w_mem_v'] = nrm((DEPTH, D_MODEL, MEM_W), D_MODEL ** -0.5)
    inp['mem_q_g'] = gain((DEPTH, MEM_HD))
    inp['mem_k_g'] = gain((DEPTH, MEM_HD))
    inp['w_mem_o'] = nrm((DEPTH, MEM_W, D_MODEL), MEM_W ** -0.5)
    inp['w_out'] = nrm((DEPTH, D_MODEL, D_MODEL), D_MODEL ** -0.5)
    inp['norm2_g'] = gain((DEPTH, D_MODEL))
    inp['peer_w_q'] = nrm((DEPTH, D_MODEL, PEER_HEADS * PEER_DKEY), D_MODEL ** -0.5)
    inp['peer_k1'] = nrm((DEPTH, PEER_NKEYS, PEER_DKEY // 2), (PEER_DKEY // 2) ** -0.5)
    inp['peer_k2'] = nrm((DEPTH, PEER_NKEYS, PEER_DKEY // 2), (PEER_DKEY // 2) ** -0.5)
    inp['peer_u'] = nrm((DEPTH, PEER_N, D_MODEL), D_MODEL ** -0.5)
    inp['peer_v'] = nrm((DEPTH, PEER_N, D_MODEL), PEER_HEADS ** -0.5)
    return inp


def reference(x_prompt, x_sample, mem_prompt, state_conv, state_sw0_k, state_sw0_v, state_sw1_k, state_sw1_v,
              state_sw2_k, state_sw2_v, cache_mem_k, cache_mem_v, norm1_g, mem_norm_g, w_in, conv_w, conv_b,
              conv_ln_g, conv_ln_b, w_conv_o, sw_q_g, sw_k_g, w_sw_o, w_mem_k, w_mem_v, mem_q_g, mem_k_g, w_mem_o,
              w_out, norm2_g, peer_w_q, peer_k1, peer_k2, peer_u, peer_v):
    sw_bufs_all = (state_sw0_k, state_sw0_v, state_sw1_k, state_sw1_v, state_sw2_k, state_sw2_v)
    yp, ys = x_prompt, x_sample
    conv_p, conv_s, memk_p, memv_p = [], [], [], []
    sw_p = [[] for _ in range(2 * SW_GROUPS)]
    sw_s = [[] for _ in range(2 * SW_GROUPS)]
    for l in range(DEPTH):
        lw = (norm1_g[l], w_in[l], conv_w[l], conv_b[l], conv_ln_g[l], conv_ln_b[l], w_conv_o[l], sw_q_g[l],
              sw_k_g[l], w_sw_o[l], mem_q_g[l], w_mem_o[l], w_out[l], norm2_g[l], peer_w_q[l], peer_k1[l],
              peer_k2[l], peer_u[l], peer_v[l])
        mk, mv = memory_kv(mem_prompt, mem_norm_g[l], w_mem_k[l], w_mem_v[l], mem_k_g[l])
        zeros_prev = jnp.zeros((yp.shape[0], CONV_W - 1, C_CONV), yp.dtype)
        yp, cp, sp = layer(yp, zeros_prev, None, mk, mv, *lw)
        ys, cs, ss = layer(ys, state_conv[l], [b[l] for b in sw_bufs_all], cache_mem_k[l], cache_mem_v[l], *lw)
        conv_p.append(cp)
        conv_s.append(cs)
        memk_p.append(mk)
        memv_p.append(mv)
        for i in range(2 * SW_GROUPS):
            sw_p[i].append(sp[i])
            sw_s[i].append(ss[i])
    swp = [jnp.stack(a, axis=0) for a in sw_p]
    sws = [jnp.stack(a, axis=0) for a in sw_s]
    return (yp, ys, jnp.stack(conv_p, axis=0), jnp.stack(conv_s, axis=0),
            swp[0], swp[1], swp[2], swp[3], swp[4], swp[5],
            sws[0], sws[1], sws[2], sws[3], sws[4], sws[5],
            jnp.stack(memk_p, axis=0), jnp.stack(memv_p, axis=0))
```

```python
import functools
import math

import numpy as np
import jax
import jax.numpy as jnp
from jax import lax
from jax.experimental import pallas as pl
from jax.experimental.pallas import tpu as pltpu

F32 = jnp.float32
BF16 = jnp.bfloat16
EPS = 1e-6
NEG_INF = float("-inf")

LANES = 128
SUBLANES = 8
VMEM_LIMIT = 56 * 1024 * 1024

D_MODEL = 2048
C_CONV = 1024
CONV_W = 31
CONV_PAD = 32
SW_GROUPS = 3
SW_DILATIONS = (1, 4, 16)
SW_WINDOWS = (128, 512, 2048)
SW_SPAN = 128
SW_HG = 4
SW_HD = 128
SW_OUT = SW_HG * SW_HD
MEM_H = 4
MEM_HD = 256
MEM_W = MEM_H * MEM_HD
IN_CONV = 2 * C_CONV
IN_SW = SW_GROUPS * 3 * SW_OUT
IN_COLS = IN_CONV + IN_SW + MEM_W + 3 * D_MODEL
COL_SW = IN_CONV
COL_MEMQ = IN_CONV + IN_SW
COL_GATE = COL_MEMQ + MEM_W
PEER_HEADS = 8
PEER_NKEYS = 128
PEER_DKEY = 256
PEER_TOPK = 16


def _alibi_slopes():
    n = SW_GROUPS * SW_HG
    s = np.float32(2.0) ** (np.float32(-8.0) * np.arange(1, n + 1, dtype=np.float32) / np.float32(n))
    return s.reshape(SW_GROUPS, SW_HG)


def _rms(x, g):
    return x * lax.rsqrt(jnp.mean(x * x, axis=-1, keepdims=True) + EPS) * g


def _dot(a, b):
    return jnp.dot(a, b, preferred_element_type=F32)


def _dot_nt(a, b):
    return lax.dot_general(a, b, (((1,), (1,)), ((), ())), preferred_element_type=F32)


def _params(*sem):
    return pltpu.CompilerParams(dimension_semantics=sem, vmem_limit_bytes=VMEM_LIMIT)


def _pick(n, options):
    for o in options:
        if n % o == 0:
            return o
    raise ValueError(f"no tile in {options} divides {n}")


def _norm_matmul_kernel(x_ref, g_ref, w_ref, o_ref, xn_ref):
    @pl.when(pl.program_id(1) == 0)
    def _():
        xn_ref[...] = _rms(x_ref[...], g_ref[...]).astype(BF16)

    o_ref[...] = _dot(xn_ref[...], w_ref[...])


def norm_matmul(x, g, w_bf):
    n, d = x.shape
    cols = w_bf.shape[1]
    tm = _pick(n, (512, 256))
    tn = _pick(cols, (1536, 1024))
    return pl.pallas_call(
        _norm_matmul_kernel,
        grid=(n // tm, cols // tn),
        in_specs=[pl.BlockSpec((tm, d), lambda i, j: (i, 0)),
                  pl.BlockSpec((1, d), lambda i, j: (0, 0)),
                  pl.BlockSpec((d, tn), lambda i, j: (0, j))],
        out_specs=pl.BlockSpec((tm, tn), lambda i, j: (i, j)),
        out_shape=jax.ShapeDtypeStruct((n, cols), F32),
        scratch_shapes=[pltpu.VMEM((tm, d), BF16)],
        compiler_params=_params("parallel", "arbitrary"),
        name="norm_matmul",
    )(x, g.reshape(1, d), w_bf)


def _matmul_res_kernel(a_ref, w_ref, x_ref, o_ref):
    o_ref[...] = x_ref[...] + _dot(a_ref[...], w_ref[...])


def matmul_residual(a_bf, w_bf, x):
    n, k = a_bf.shape
    cols = w_bf.shape[1]
    tm = _pick(n, (512, 256))
    tn = _pick(cols, (1024,))
    return pl.pallas_call(
        _matmul_res_kernel,
        grid=(n // tm, cols // tn),
        in_specs=[pl.BlockSpec((tm, k), lambda i, j: (i, 0)),
                  pl.BlockSpec((k, tn), lambda i, j: (0, j)),
                  pl.BlockSpec((tm, tn), lambda i, j: (i, j))],
        out_specs=pl.BlockSpec((tm, tn), lambda i, j: (i, j)),
        out_shape=jax.ShapeDtypeStruct((n, cols), F32),
        compiler_params=_params("parallel", "arbitrary"),
        name="matmul_residual",
    )(a_bf, w_bf, x)


def _headnorm_kernel(x_ref, g_ref, o_ref, *, heads, hd):
    for h in range(heads):
        sl = slice(h * hd, (h + 1) * hd)
        o_ref[:, sl] = _rms(x_ref[:, sl], g_ref[...])


def headnorm(x, g, heads, hd, col_block):
    n = x.shape[0]
    w = heads * hd
    tm = _pick(n, (512, 256))
    return pl.pallas_call(
        functools.partial(_headnorm_kernel, heads=heads, hd=hd),
        grid=(n // tm,),
        in_specs=[pl.BlockSpec((tm, w), lambda i: (i, col_block)),
                  pl.BlockSpec((1, hd), lambda i: (0, 0))],
        out_specs=pl.BlockSpec((tm, w), lambda i: (i, 0)),
        out_shape=jax.ShapeDtypeStruct((n, w), F32),
        compiler_params=_params("parallel"),
        name="headnorm",
    )(x, g.reshape(1, hd))


def _conv_kernel(hc_ref, prev_ref, cw_ref, cb_ref, lg_ref, lb_ref, act_ref, st_ref, ext_ref, c_ref, *, tt):
    i = pl.program_id(1)
    hist = CONV_W - 1
    lo = CONV_PAD - hist

    @pl.when(i == 0)
    def _():
        ext_ref[lo:CONV_PAD, :] = prev_ref[0]

    @pl.when(i > 0)
    def _():
        ext_ref[lo:CONV_PAD, :] = ext_ref[tt + lo:tt + CONV_PAD, :]

    a = hc_ref[0, :, :C_CONV]
    b = hc_ref[0, :, C_CONV:]
    ext_ref[CONV_PAD:CONV_PAD + tt, :] = a * jax.nn.sigmoid(b)

    for c in range(C_CONV // LANES):
        cs = slice(c * LANES, (c + 1) * LANES)
        acc = jnp.zeros((tt, LANES), F32)
        for w in range(CONV_W):
            acc = acc + ext_ref[lo + w:lo + w + tt, cs] * cw_ref[w:w + 1, cs]
        c_ref[:, cs] = acc + cb_ref[:, cs]

    cv = c_ref[...]
    mu = jnp.mean(cv, axis=-1, keepdims=True)
    xc = cv - mu
    var = jnp.mean(xc * xc, axis=-1, keepdims=True)
    y = xc * lax.rsqrt(var + EPS) * lg_ref[...] + lb_ref[...]
    act_ref[0] = (y * jax.nn.sigmoid(y)).astype(BF16)

    @pl.when(i == pl.num_programs(1) - 1)
    def _():
        st_ref[0] = ext_ref[tt + lo:tt + CONV_PAD, :]


def conv_branch(h3, prev, conv_w, conv_b, ln_g, ln_b):
    b, t, _ = h3.shape
    tt = _pick(t, (128, 8))
    hist = CONV_W - 1
    return pl.pallas_call(
        functools.partial(_conv_kernel, tt=tt),
        grid=(b, t // tt),
        in_specs=[pl.BlockSpec((1, tt, IN_CONV), lambda bi, i: (bi, i, 0)),
                  pl.BlockSpec((1, hist, C_CONV), lambda bi, i: (bi, 0, 0)),
                  pl.BlockSpec((CONV_W, C_CONV), lambda bi, i: (0, 0)),
                  pl.BlockSpec((1, C_CONV), lambda bi, i: (0, 0)),
                  pl.BlockSpec((1, C_CONV), lambda bi, i: (0, 0)),
                  pl.BlockSpec((1, C_CONV), lambda bi, i: (0, 0))],
        out_specs=[pl.BlockSpec((1, tt, C_CONV), lambda bi, i: (bi, i, 0)),
                   pl.BlockSpec((1, hist, C_CONV), lambda bi, i: (bi, 0, 0))],
        out_shape=[jax.ShapeDtypeStruct((b, t, C_CONV), BF16),
                   jax.ShapeDtypeStruct((b, hist, C_CONV), F32)],
        scratch_shapes=[pltpu.VMEM((CONV_PAD + tt, C_CONV), F32),
                        pltpu.VMEM((tt, C_CONV), F32)],
        compiler_params=_params("parallel", "arbitrary"),
        name="conv_branch",
    )(h3, prev, conv_w, conv_b.reshape(1, C_CONV), ln_g.reshape(1, C_CONV), ln_b.reshape(1, C_CONV))


def _sw_prompt_kernel(q_ref, kc_ref, kp_ref, vc_ref, vp_ref, qg_ref, kg_ref, o_ref, lse_ref, kn_ref,
                      *, r, slopes):
    n = pl.program_id(2)
    blk = SW_SPAN
    ii = lax.broadcasted_iota(jnp.int32, (blk, blk), 0)
    cc = lax.broadcasted_iota(jnp.int32, (blk, blk), 1)
    back_c = ii - cc
    back_p = back_c + blk
    dist_c = (r * back_c).astype(F32)
    dist_p = (r * back_p).astype(F32)
    first = jnp.where(n >= 1, 0.0, NEG_INF).astype(F32)
    qg = qg_ref[...]
    kg = kg_ref[...]
    for h in range(SW_HG):
        sl = slice(h * SW_HD, (h + 1) * SW_HD)
        qn = (_rms(q_ref[0, :, sl], qg) * (SW_HD ** -0.5)).astype(BF16)
        kcn = _rms(kc_ref[0, :, sl], kg)
        kpn = _rms(kp_ref[0, :, sl], kg)
        kn_ref[0, :, sl] = kcn
        s_c = _dot_nt(qn, kcn.astype(BF16))
        s_p = _dot_nt(qn, kpn.astype(BF16))
        s_c = jnp.where(back_c >= 0, s_c - slopes[h] * dist_c, NEG_INF)
        s_p = jnp.where(back_p <= SW_SPAN, s_p - slopes[h] * dist_p + first, NEG_INF)
        m = jnp.maximum(jnp.max(s_c, axis=-1, keepdims=True), jnp.max(s_p, axis=-1, keepdims=True))
        p_c = jnp.exp(s_c - m)
        p_p = jnp.exp(s_p - m)
        l = jnp.sum(p_c, axis=-1, keepdims=True) + jnp.sum(p_p, axis=-1, keepdims=True)
        o = _dot(p_c.astype(BF16), vc_ref[0, :, sl].astype(BF16)) + _dot(p_p.astype(BF16), vp_ref[0, :, sl].astype(BF16))
        o_ref[0, :, sl] = o / l
        lse_ref[0, :, sl] = jnp.broadcast_to(m + jnp.log(l), (blk, SW_HD))


def sw_prompt_group(h3, g, qg, kg):
    b, t, _ = h3.shape
    r = SW_DILATIONS[g]
    u = t // r
    nb = u // SW_SPAN
    hr = h3.reshape(b, u, r * IN_COLS)
    cpb = IN_COLS // SW_OUT
    base = COL_SW // SW_OUT + 3 * g
    blk = (1, SW_SPAN, SW_OUT)
    slopes = tuple(float(s) for s in _alibi_slopes()[g])

    def col(off):
        return lambda bi, rho, n: (bi, n, rho * cpb + base + off)

    def col_prev(off):
        return lambda bi, rho, n: (bi, jnp.maximum(n - 1, 0), rho * cpb + base + off)

    out_spec = pl.BlockSpec(blk, lambda bi, rho, n: (bi, n, rho))
    out_sds = jax.ShapeDtypeStruct((b, u, r * SW_OUT), F32)
    o, lse, kn = pl.pallas_call(
        functools.partial(_sw_prompt_kernel, r=r, slopes=slopes),
        grid=(b, r, nb),
        in_specs=[pl.BlockSpec(blk, col(0)),
                  pl.BlockSpec(blk, col(1)), pl.BlockSpec(blk, col_prev(1)),
                  pl.BlockSpec(blk, col(2)), pl.BlockSpec(blk, col_prev(2)),
                  pl.BlockSpec((1, SW_HD), lambda bi, rho, n: (0, 0)),
                  pl.BlockSpec((1, SW_HD), lambda bi, rho, n: (0, 0))],
        out_specs=[out_spec, out_spec, out_spec],
        out_shape=[out_sds, out_sds, out_sds],
        compiler_params=_params("parallel", "parallel", "arbitrary"),
        name=f"sw_prompt_g{g}",
    )(hr, hr, hr, hr, hr, qg.reshape(1, SW_HD), kg.reshape(1, SW_HD))
    shp = (b, t, SW_OUT)
    return o.reshape(shp), lse.reshape(shp), kn.reshape(shp)


def _sw_sample_kernel(q_ref, kn_ref, vn_ref, kb_ref, vb_ref, qg_ref, kg_ref, o_ref, lse_ref, kno_ref,
                      *, r, buf_len, t_new, slopes):
    tq = lax.broadcasted_iota(jnp.int32, (t_new, buf_len), 0)
    pk = lax.broadcasted_iota(jnp.int32, (t_new, buf_len), 1)
    dist_b = buf_len + tq - pk
    ok_b = ((dist_b & (r - 1)) == 0) & (dist_b <= SW_SPAN * r)
    dist_bf = dist_b.astype(F32)
    tq2 = lax.broadcasted_iota(jnp.int32, (t_new, t_new), 0)
    tk2 = lax.broadcasted_iota(jnp.int32, (t_new, t_new), 1)
    dist_n = tq2 - tk2
    ok_n = (dist_n >= 0) & ((dist_n & (r - 1)) == 0)
    dist_nf = dist_n.astype(F32)
    qg = qg_ref[...]
    kg = kg_ref[...]
    for h in range(SW_HG):
        sl = slice(h * SW_HD, (h + 1) * SW_HD)
        qn = (_rms(q_ref[0, :, sl], qg) * (SW_HD ** -0.5)).astype(BF16)
        knn = _rms(kn_ref[0, :, sl], kg)
        kno_ref[0, :, sl] = knn
        s_b = _dot_nt(qn, kb_ref[0, :, sl].astype(BF16))
        s_n = _dot_nt(qn, knn.astype(BF16))
        s_b = jnp.where(ok_b, s_b - slopes[h] * dist_bf, NEG_INF)
        s_n = jnp.where(ok_n, s_n - slopes[h] * dist_nf, NEG_INF)
        m = jnp.maximum(jnp.max(s_b, axis=-1, keepdims=True), jnp.max(s_n, axis=-1, keepdims=True))
        p_b = jnp.exp(s_b - m)
        p_n = jnp.exp(s_n - m)
        l = jnp.sum(p_b, axis=-1, keepdims=True) + jnp.sum(p_n, axis=-1, keepdims=True)
        o = _dot(p_b.astype(BF16), vb_ref[0, :, sl].astype(BF16)) + _dot(p_n.astype(BF16), vn_ref[0, :, sl].astype(BF16))
        o_ref[0, :, sl] = o / l
        lse_ref[0, :, sl] = jnp.broadcast_to(m + jnp.log(l), (t_new, SW_HD))


def sw_sample_group(h3, g, k_buf, v_buf, qg, kg):
    b, t, _ = h3.shape
    buf_len = k_buf.shape[1]
    r = SW_DILATIONS[g]
    assert buf_len >= SW_SPAN * r and t <= r * SW_SPAN
    base = COL_SW // SW_OUT + 3 * g
    blk = (1, t, SW_OUT)
    bblk = (1, buf_len, SW_OUT)
    slopes = tuple(float(s) for s in _alibi_slopes()[g])
    out_spec = pl.BlockSpec(blk, lambda bi: (bi, 0, 0))
    out_sds = jax.ShapeDtypeStruct((b, t, SW_OUT), F32)
    return pl.pallas_call(
        functools.partial(_sw_sample_kernel, r=r, buf_len=buf_len, t_new=t, slopes=slopes),
        grid=(b,),
        in_specs=[pl.BlockSpec(blk, lambda bi: (bi, 0, base)),
                  pl.BlockSpec(blk, lambda bi: (bi, 0, base + 1)),
                  pl.BlockSpec(blk, lambda bi: (bi, 0, base + 2)),
                  pl.BlockSpec(bblk, lambda bi: (bi, 0, 0)),
                  pl.BlockSpec(bblk, lambda bi: (bi, 0, 0)),
                  pl.BlockSpec((1, SW_HD), lambda bi: (0, 0)),
                  pl.BlockSpec((1, SW_HD), lambda bi: (0, 0))],
        out_specs=[out_spec, out_spec, out_spec],
        out_shape=[out_sds, out_sds, out_sds],
        compiler_params=_params("parallel"),
        name=f"sw_sample_g{g}",
    )(h3, h3, h3, k_buf, v_buf, qg.reshape(1, SW_HD), kg.reshape(1, SW_HD))


def _mem_attn_kernel(q_ref, g_ref, k_ref, v_ref, o_ref):
    qn = (_rms(q_ref[0], g_ref[...]) * (MEM_HD ** -0.5)).astype(BF16)
    s = _dot_nt(qn, k_ref[0].astype(BF16))
    m = jnp.max(s, axis=-1, keepdims=True)
    p = jnp.exp(s - m)
    p = p / jnp.sum(p, axis=-1, keepdims=True)
    o_ref[0] = _dot(p.astype(BF16), v_ref[0].astype(BF16)).astype(BF16)


def mem_attention(h3, q_g, mem_k, mem_v):
    b, t, _ = h3.shape
    n_mem = mem_k.shape[1]
    tq = _pick(t, (512, 8))
    qbase = COL_MEMQ // MEM_HD
    return pl.pallas_call(
        _mem_attn_kernel,
        grid=(b, t // tq, MEM_H),
        in_specs=[pl.BlockSpec((1, tq, MEM_HD), lambda bi, i, h: (bi, i, qbase + h)),
                  pl.BlockSpec((1, MEM_HD), lambda bi, i, h: (0, 0)),
                  pl.BlockSpec((1, n_mem, MEM_HD), lambda bi, i, h: (bi, 0, h)),
                  pl.BlockSpec((1, n_mem, MEM_HD), lambda bi, i, h: (bi, 0, h))],
        out_specs=pl.BlockSpec((1, tq, MEM_HD), lambda bi, i, h: (bi, i, h)),
        out_shape=jax.ShapeDtypeStruct((b, t, MEM_W), BF16),
        compiler_params=_params("parallel", "parallel", "arbitrary"),
        name="mem_attention",
    )(h3, q_g.reshape(1, MEM_HD), mem_k, mem_v)


def _merge_kernel(ca_ref, o0_ref, o1_ref, o2_ref, l0_ref, l1_ref, l2_ref, mo_ref, g0_ref, g1_ref, g2_ref,
                  wc_ref, ws_ref, wm_ref, out_ref):
    l0, l1, l2 = l0_ref[...], l1_ref[...], l2_ref[...]
    m = jnp.maximum(jnp.maximum(l0, l1), l2)
    e0, e1, e2 = jnp.exp(l0 - m), jnp.exp(l1 - m), jnp.exp(l2 - m)
    sw = (e0 * o0_ref[...] + e1 * o1_ref[...] + e2 * o2_ref[...]) / (e0 + e1 + e2)
    conv_out = _dot(ca_ref[...], wc_ref[...])
    sw_out = _dot(sw.astype(BF16), ws_ref[...])
    mem_out = _dot(mo_ref[...], wm_ref[...])
    merged = (jax.nn.sigmoid(g0_ref[...]) * conv_out + jax.nn.sigmoid(g1_ref[...]) * sw_out
              + jax.nn.sigmoid(g2_ref[...]) * mem_out)
    out_ref[...] = merged.astype(BF16)


def merge_branches(hcat, c_act, sw_o, sw_lse, mem_o, w_conv_o, w_sw_o, w_mem_o):
    n = hcat.shape[0]
    tm = _pick(n, (512, 256))
    tn = 512
    gbase = COL_GATE // tn
    gstep = D_MODEL // tn

    def row(width):
        return pl.BlockSpec((tm, width), lambda i, j: (i, 0))

    def gate(br):
        return pl.BlockSpec((tm, tn), lambda i, j: (i, gbase + br * gstep + j))

    def wcol(k):
        return pl.BlockSpec((k, tn), lambda i, j: (0, j))

    return pl.pallas_call(
        _merge_kernel,
        grid=(n // tm, D_MODEL // tn),
        in_specs=[row(C_CONV), row(SW_OUT), row(SW_OUT), row(SW_OUT), row(SW_OUT), row(SW_OUT), row(SW_OUT),
                  row(MEM_W), gate(0), gate(1), gate(2), wcol(C_CONV), wcol(SW_OUT), wcol(MEM_W)],
        out_specs=pl.BlockSpec((tm, tn), lambda i, j: (i, j)),
        out_shape=jax.ShapeDtypeStruct((n, D_MODEL), BF16),
        compiler_params=_params("parallel", "arbitrary"),
        name="merge_branches",
    )(c_act, *sw_o, *sw_lse, mem_o, hcat, hcat, hcat, w_conv_o, w_sw_o, w_mem_o)


def _top_values(s, rows16, k):
    vals = jnp.full(rows16.shape, NEG_INF, F32)
    work = s
    mx = None
    for i in range(k):
        mx = jnp.max(work, axis=0, keepdims=True)
        vals = jnp.where(rows16 == i, mx, vals)
        work = jnp.where(work == mx, NEG_INF, work)
    return vals, mx


def _router_kernel(q_ref, k1_ref, k2_ref, s1_ref, s2_ref, e1_ref, e2_ref, tau_ref, *, tm):
    half = PEER_DKEY // 2
    rows16 = lax.broadcasted_iota(jnp.int32, (PEER_TOPK, tm), 0)
    rows8 = lax.broadcasted_iota(jnp.int32, (SUBLANES, tm), 0)
    k1 = k1_ref[...]
    k2 = k2_ref[...]

    def head(h, carry):
        c0 = pl.multiple_of(h * PEER_DKEY, PEER_DKEY)
        qa = q_ref[:, pl.ds(c0, half)].astype(BF16)
        qb = q_ref[:, pl.ds(c0 + half, half)].astype(BF16)
        s1 = _dot_nt(k1, qa)
        s2 = _dot_nt(k2, qb)
        s1 = s1 - jnp.max(s1, axis=0, keepdims=True)
        s2 = s2 - jnp.max(s2, axis=0, keepdims=True)
        v1, _ = _top_values(s1, rows16, PEER_TOPK)
        v2, _ = _top_values(s2, rows16, PEER_TOPK)
        parts = [v1[0:1] + v2]
        v2_top = v2[0:SUBLANES]
        for i in range(1, PEER_TOPK):
            keep = PEER_TOPK // (i + 1)
            parts.append(jnp.where(rows8 < keep, v1[i:i + 1] + v2_top, NEG_INF))
        cand = jnp.concatenate(parts, axis=0)
        work = cand
        tau = None
        for _i in range(PEER_TOPK):
            tau = jnp.max(work, axis=0, keepdims=True)
            work = jnp.where(work == tau, NEG_INF, work)
        z = jnp.sum(jnp.where(cand >= tau, jnp.exp(cand), 0.0), axis=0, keepdims=True)
        s1_ref[h] = s1
        s2_ref[h] = s2
        e1_ref[h] = jnp.exp(s1) * (1.0 / z)
        e2_ref[h] = jnp.exp(s2)
        tau_ref[pl.ds(h, 1), :] = tau
        return carry

    lax.fori_loop(0, PEER_HEADS, head, 0)


def peer_router(q, k1_bf, k2_bf):
    n = q.shape[0]
    tm = 256
    big = pl.BlockSpec((PEER_HEADS, PEER_NKEYS, tm), lambda i: (0, 0, i))
    big_sds = jax.ShapeDtypeStruct((PEER_HEADS, PEER_NKEYS, n), F32)
    return pl.pallas_call(
        functools.partial(_router_kernel, tm=tm),
        grid=(n // tm,),
        in_specs=[pl.BlockSpec((tm, PEER_HEADS * PEER_DKEY), lambda i: (i, 0)),
                  pl.BlockSpec((PEER_NKEYS, PEER_DKEY // 2), lambda i: (0, 0)),
                  pl.BlockSpec((PEER_NKEYS, PEER_DKEY // 2), lambda i: (0, 0))],
        out_specs=[big, big, big, big, pl.BlockSpec((PEER_HEADS, tm), lambda i: (0, i))],
        out_shape=[big_sds, big_sds, big_sds, big_sds, jax.ShapeDtypeStruct((PEER_HEADS, n), F32)],
        compiler_params=_params("parallel"),
        name="peer_router",
    )(q, k1_bf, k2_bf)


def _gelu_tanh(x):
    return 0.5 * x * (1.0 + jnp.tanh(math.sqrt(2.0 / math.pi) * (x + 0.044715 * (x * x * x))))


def _peer_kernel(h_ref, g2_ref, s1_ref, e1_ref, s2_ref, e2_ref, tau_ref, u_ref, vt_ref, y_ref,
                 hn_ref, acc_ref, act_ref, coef_ref, *, tm, te):
    j = pl.program_id(1)

    @pl.when(j == 0)
    def _():
        hn_ref[...] = _rms(h_ref[...], g2_ref[...]).astype(BF16)
        acc_ref[...] = jnp.zeros_like(acc_ref)

    act_ref[...] = _dot_nt(u_ref[...], hn_ref[...])
    rows = te // PEER_NKEYS
    for al in range(rows):
        rs = slice(al * PEER_NKEYS, (al + 1) * PEER_NKEYS)
        for c in range(tm // LANES):
            cs = slice(c * LANES, (c + 1) * LANES)
            gate = jnp.zeros((PEER_NKEYS, LANES), F32)
            for h in range(PEER_HEADS):
                s1c = s1_ref[h, 0, al:al + 1, cs]
                e1c = e1_ref[h, 0, al:al + 1, cs]
                tau = tau_ref[h:h + 1, cs]
                total = s1c + s2_ref[h, :, cs]
                gate = gate + jnp.where(total >= tau, e2_ref[h, :, cs] * e1c, 0.0)
            coef_ref[rs, cs] = (gate * _gelu_tanh(act_ref[rs, cs])).astype(BF16)
    acc_ref[...] += _dot(vt_ref[...], coef_ref[...])

    @pl.when(j == pl.num_programs(1) - 1)
    def _():
        y_ref[...] = h_ref[...] + acc_ref[...].T


def peer_experts(h, norm2_g, routed, u_bf, vt_bf):
    n, d = h.shape
    n_exp = u_bf.shape[0]
    tm = _pick(n, (512, 256))
    te = 512
    s1, s2, e1, e2, tau = routed
    rows = te // PEER_NKEYS
    s1 = s1.reshape(PEER_HEADS, PEER_NKEYS // rows, rows, n)
    e1 = e1.reshape(PEER_HEADS, PEER_NKEYS // rows, rows, n)
    row = pl.BlockSpec((PEER_HEADS, 1, rows, tm), lambda i, j: (0, j, 0, i))
    big = pl.BlockSpec((PEER_HEADS, PEER_NKEYS, tm), lambda i, j: (0, 0, i))
    return pl.pallas_call(
        functools.partial(_peer_kernel, tm=tm, te=te),
        grid=(n // tm, n_exp // te),
        in_specs=[pl.BlockSpec((tm, d), lambda i, j: (i, 0)),
                  pl.BlockSpec((1, d), lambda i, j: (0, 0)),
                  row, row, big, big,
                  pl.BlockSpec((PEER_HEADS, tm), lambda i, j: (0, i)),
                  pl.BlockSpec((te, d), lambda i, j: (j, 0)),
                  pl.BlockSpec((d, te), lambda i, j: (0, j))],
        out_specs=pl.BlockSpec((tm, d), lambda i, j: (i, 0)),
        out_shape=jax.ShapeDtypeStruct((n, d), F32),
        scratch_shapes=[pltpu.VMEM((tm, d), BF16),
                        pltpu.VMEM((d, tm), F32),
                        pltpu.VMEM((te, tm), F32),
                        pltpu.VMEM((te, tm), BF16)],
        compiler_params=_params("parallel", "arbitrary"),
        name="peer_experts",
    )(h, norm2_g.reshape(1, d), s1, e1, s2, e2, tau, u_bf, vt_bf)


def _layer(x, conv_prev, sw_bufs, mem_k, mem_v, w):
    b, t, d = x.shape
    n = b * t
    x2 = x.reshape(n, d)
    hcat = norm_matmul(x2, w["norm1_g"], w["w_in"])
    h3 = hcat.reshape(b, t, IN_COLS)
    c_act, conv_state = conv_branch(h3, conv_prev, w["conv_w"], w["conv_b"], w["conv_ln_g"], w["conv_ln_b"])
    sw_o, sw_lse, kv_new = [], [], []
    for g in range(SW_GROUPS):
        if sw_bufs is None:
            o, lse, kn = sw_prompt_group(h3, g, w["sw_q_g"][g], w["sw_k_g"][g])
        else:
            o, lse, kn = sw_sample_group(h3, g, sw_bufs[2 * g], sw_bufs[2 * g + 1], w["sw_q_g"][g], w["sw_k_g"][g])
        sw_o.append(o.reshape(n, SW_OUT))
        sw_lse.append(lse.reshape(n, SW_OUT))
        vcol = COL_SW + (3 * g + 2) * SW_OUT
        kv_new += [kn, h3[:, :, vcol:vcol + SW_OUT]]
    mem_o = mem_attention(h3, w["mem_q_g"], mem_k, mem_v)
    merged = merge_branches(hcat, c_act.reshape(n, C_CONV), sw_o, sw_lse, mem_o.reshape(n, MEM_W),
                            w["w_conv_o"], w["w_sw_o"], w["w_mem_o"])
    hres = matmul_residual(merged, w["w_out"], x2)
    q = norm_matmul(hres, w["norm2_g"], w["peer_w_q"])
    routed = peer_router(q, w["peer_k1"], w["peer_k2"])
    y = peer_experts(hres, w["norm2_g"], routed, w["peer_u"], w["peer_vt"])
    return y.reshape(b, t, d), conv_state, kv_new


def kernel(x_prompt, x_sample, mem_prompt, state_conv, state_sw0_k, state_sw0_v, state_sw1_k, state_sw1_v, state_sw2_k, state_sw2_v, cache_mem_k, cache_mem_v, norm1_g, mem_norm_g, w_in, conv_w, conv_b, conv_ln_g, conv_ln_b, w_conv_o, sw_q_g, sw_k_g, w_sw_o, w_mem_k, w_mem_v, mem_q_g, mem_k_g, w_mem_o, w_out, norm2_g, peer_w_q, peer_k1, peer_k2, peer_u, peer_v):
    depth = w_in.shape[0]
    bp, tp, _ = x_prompt.shape
    bs, ts, _ = x_sample.shape
    n_mem = mem_prompt.shape[1]
    sw_states = (state_sw0_k, state_sw0_v, state_sw1_k, state_sw1_v, state_sw2_k, state_sw2_v)
    yp, ys = x_prompt, x_sample
    conv_p, conv_s, memk_p, memv_p = [], [], [], []
    sw_p = [[] for _ in range(2 * SW_GROUPS)]
    sw_s = [[] for _ in range(2 * SW_GROUPS)]
    for l in range(depth):
        w = dict(
            norm1_g=norm1_g[l], w_in=w_in[l].astype(BF16), conv_w=conv_w[l], conv_b=conv_b[l],
            conv_ln_g=conv_ln_g[l], conv_ln_b=conv_ln_b[l], w_conv_o=w_conv_o[l].astype(BF16),
            sw_q_g=sw_q_g[l], sw_k_g=sw_k_g[l], w_sw_o=w_sw_o[l].astype(BF16), mem_q_g=mem_q_g[l],
            w_mem_o=w_mem_o[l].astype(BF16), w_out=w_out[l].astype(BF16), norm2_g=norm2_g[l],
            peer_w_q=peer_w_q[l].astype(BF16), peer_k1=peer_k1[l].astype(BF16), peer_k2=peer_k2[l].astype(BF16),
            peer_u=peer_u[l].astype(BF16), peer_vt=peer_v[l].T.astype(BF16))
        w_kv = jnp.concatenate([w_mem_k[l], w_mem_v[l]], axis=1).astype(BF16)
        kv = norm_matmul(mem_prompt.reshape(bp * n_mem, D_MODEL), mem_norm_g[l], w_kv)
        mk = headnorm(kv, mem_k_g[l], MEM_H, MEM_HD, 0).reshape(bp, n_mem, MEM_W)
        mv = kv[:, MEM_W:].reshape(bp, n_mem, MEM_W)

        zeros_prev = jnp.zeros((bp, CONV_W - 1, C_CONV), F32)
        yp, cp, kvp = _layer(yp, zeros_prev, None, mk, mv, w)
        bufs = [s[l].reshape(bs, s.shape[2], SW_OUT) for s in sw_states]
        ys, cs, kvs = _layer(ys, state_conv[l], bufs,
                             cache_mem_k[l].reshape(bs, n_mem, MEM_W), cache_mem_v[l].reshape(bs, n_mem, MEM_W), w)
        conv_p.append(cp)
        conv_s.append(cs)
        memk_p.append(mk.reshape(bp, n_mem, MEM_H, MEM_HD))
        memv_p.append(mv.reshape(bp, n_mem, MEM_H, MEM_HD))
        for i in range(2 * SW_GROUPS):
            keep = min(SW_WINDOWS[i // 2], tp)
            sw_p[i].append(kvp[i][:, tp - keep:].reshape(bp, keep, SW_HG, SW_HD))
            new = jnp.concatenate([bufs[i], kvs[i]], axis=1)[:, ts:]
            sw_s[i].append(new.reshape(bs, new.shape[1], SW_HG, SW_HD))
    swp = [jnp.stack(a, axis=0) for a in sw_p]
    sws = [jnp.stack(a, axis=0) for a in sw_s]
    return (yp, ys, jnp.stack(conv_p, axis=0), jnp.stack(conv_s, axis=0),
            swp[0], swp[1], swp[2], swp[3], swp[4], swp[5],
            sws[0], sws[1], sws[2], sws[3], sws[4], sws[5],
            jnp.stack(memk_p, axis=0), jnp.stack(memv_p, axis=0))
```

```python
import functools
import math

import numpy as np
import jax
import jax.numpy as jnp
from jax import lax
from jax.experimental import pallas as pl
from jax.experimental.pallas import tpu as pltpu

F32 = jnp.float32
BF16 = jnp.bfloat16
EPS = 1e-6
NEG_INF = float("-inf")

LANES = 128
SUBLANES = 8
VMEM_LIMIT = 56 * 1024 * 1024

D_MODEL = 2048
C_CONV = 1024
CONV_W = 31
CONV_PAD = 32
SW_GROUPS = 3
SW_DILATIONS = (1, 4, 16)
SW_WINDOWS = (128, 512, 2048)
SW_SPAN = 128
SW_HG = 4
SW_HD = 128
SW_OUT = SW_HG * SW_HD
MEM_H = 4
MEM_HD = 256
MEM_W = MEM_H * MEM_HD
IN_CONV = 2 * C_CONV
IN_SW = SW_GROUPS * 3 * SW_OUT
IN_COLS = IN_CONV + IN_SW + MEM_W + 3 * D_MODEL
COL_SW = IN_CONV
COL_MEMQ = IN_CONV + IN_SW
COL_GATE = COL_MEMQ + MEM_W
PROJ_TN = 512
PEER_HEADS = 8
PEER_NKEYS = 128
PEER_DKEY = 256
PEER_TOPK = 16

TILE_PLAIN, TILE_NORM128, TILE_NORM256 = 0, 1, 2
_TILE_HD = {TILE_NORM128: SW_HD, TILE_NORM256: MEM_HD}


def _alibi_slopes():
    n = SW_GROUPS * SW_HG
    s = np.float32(2.0) ** (np.float32(-8.0) * np.arange(1, n + 1, dtype=np.float32) / np.float32(n))
    return s.reshape(SW_GROUPS, SW_HG)


def _rms(x, g):
    return x * lax.rsqrt(jnp.mean(x * x, axis=-1, keepdims=True) + EPS) * g


def _dot(a, b):
    return jnp.dot(a, b, preferred_element_type=F32)


def _dot_nt(a, b):
    return lax.dot_general(a, b, (((1,), (1,)), ((), ())), preferred_element_type=F32)


def _params(*sem):
    return pltpu.CompilerParams(dimension_semantics=sem, vmem_limit_bytes=VMEM_LIMIT)


def _pick(n, options):
    for o in options:
        if n % o == 0:
            return o
    raise ValueError(f"no tile in {options} divides {n}")


def _norm_matmul_kernel(mode_ref, x_ref, g_ref, w_ref, cg_ref, o_ref, xn_ref, *, tn, head_modes):
    j = pl.program_id(1)

    @pl.when(j == 0)
    def _():
        xn_ref[...] = _rms(x_ref[...], g_ref[...]).astype(BF16)

    acc = _dot(xn_ref[...], w_ref[...])
    if not head_modes:
        o_ref[...] = acc
        return
    mode = mode_ref[j]

    @pl.when(mode == TILE_PLAIN)
    def _():
        o_ref[...] = acc

    for code in head_modes:
        hd = _TILE_HD[code]

        @pl.when(mode == code)
        def _():
            for h in range(tn // hd):
                sl = slice(h * hd, (h + 1) * hd)
                o_ref[:, sl] = _rms(acc[:, sl], cg_ref[:, sl])


def norm_matmul(x, g, w_bf, tile_modes=None, col_gain=None):
    n, d = x.shape
    cols = w_bf.shape[1]
    tm = _pick(n, (1024, 256))
    tn = PROJ_TN
    nt = cols // tn
    if tile_modes is None:
        tile_modes = np.zeros((nt,), np.int32)
        col_gain = jnp.ones((cols,), F32)
    head_modes = tuple(sorted(set(int(m) for m in tile_modes) - {TILE_PLAIN}))
    grid_spec = pltpu.PrefetchScalarGridSpec(
        num_scalar_prefetch=1,
        grid=(n // tm, nt),
        in_specs=[pl.BlockSpec((tm, d), lambda i, j, m: (i, 0)),
                  pl.BlockSpec((1, d), lambda i, j, m: (0, 0)),
                  pl.BlockSpec((d, tn), lambda i, j, m: (0, j)),
                  pl.BlockSpec((1, tn), lambda i, j, m: (0, j))],
        out_specs=pl.BlockSpec((tm, tn), lambda i, j, m: (i, j)),
        scratch_shapes=[pltpu.VMEM((tm, d), BF16)])
    return pl.pallas_call(
        functools.partial(_norm_matmul_kernel, tn=tn, head_modes=head_modes),
        grid_spec=grid_spec,
        out_shape=jax.ShapeDtypeStruct((n, cols), F32),
        compiler_params=_params("parallel", "arbitrary"),
        name="norm_matmul",
    )(jnp.asarray(tile_modes, jnp.int32), x, g.reshape(1, d), w_bf, col_gain.reshape(1, cols))


def _matmul_res_kernel(a_ref, w_ref, x_ref, o_ref):
    o_ref[...] = x_ref[...] + _dot(a_ref[...], w_ref[...])


def matmul_residual(a_bf, w_bf, x):
    n, k = a_bf.shape
    cols = w_bf.shape[1]
    tm = _pick(n, (512, 256))
    tn = _pick(cols, (1024,))
    return pl.pallas_call(
        _matmul_res_kernel,
        grid=(n // tm, cols // tn),
        in_specs=[pl.BlockSpec((tm, k), lambda i, j: (i, 0)),
                  pl.BlockSpec((k, tn), lambda i, j: (0, j)),
                  pl.BlockSpec((tm, tn), lambda i, j: (i, j))],
        out_specs=pl.BlockSpec((tm, tn), lambda i, j: (i, j)),
        out_shape=jax.ShapeDtypeStruct((n, cols), F32),
        compiler_params=_params("parallel", "arbitrary"),
        name="matmul_residual",
    )(a_bf, w_bf, x)


def _conv_kernel(hc_ref, prev_ref, cw_ref, cb_ref, lg_ref, lb_ref, act_ref, st_ref, ext_ref, c_ref, *, tt):
    i = pl.program_id(1)
    hist = CONV_W - 1
    lo = CONV_PAD - hist

    @pl.when(i == 0)
    def _():
        ext_ref[lo:CONV_PAD, :] = prev_ref[0]

    @pl.when(i > 0)
    def _():
        ext_ref[lo:CONV_PAD, :] = ext_ref[tt + lo:tt + CONV_PAD, :]

    a = hc_ref[0, :, :C_CONV]
    b = hc_ref[0, :, C_CONV:]
    ext_ref[CONV_PAD:CONV_PAD + tt, :] = a * jax.nn.sigmoid(b)

    for c in range(C_CONV // LANES):
        cs = slice(c * LANES, (c + 1) * LANES)
        acc = jnp.zeros((tt, LANES), F32)
        for w in range(CONV_W):
            acc = acc + ext_ref[lo + w:lo + w + tt, cs] * cw_ref[w:w + 1, cs]
        c_ref[:, cs] = acc + cb_ref[:, cs]

    cv = c_ref[...]
    mu = jnp.mean(cv, axis=-1, keepdims=True)
    xc = cv - mu
    var = jnp.mean(xc * xc, axis=-1, keepdims=True)
    y = xc * lax.rsqrt(var + EPS) * lg_ref[...] + lb_ref[...]
    act_ref[0] = (y * jax.nn.sigmoid(y)).astype(BF16)

    @pl.when(i == pl.num_programs(1) - 1)
    def _():
        st_ref[0] = ext_ref[tt + lo:tt + CONV_PAD, :]


def conv_branch(h3, prev, conv_w, conv_b, ln_g, ln_b):
    b, t, _ = h3.shape
    tt = _pick(t, (128, 8))
    hist = CONV_W - 1
    return pl.pallas_call(
        functools.partial(_conv_kernel, tt=tt),
        grid=(b, t // tt),
        in_specs=[pl.BlockSpec((1, tt, IN_CONV), lambda bi, i: (bi, i, 0)),
                  pl.BlockSpec((1, hist, C_CONV), lambda bi, i: (bi, 0, 0)),
                  pl.BlockSpec((CONV_W, C_CONV), lambda bi, i: (0, 0)),
                  pl.BlockSpec((1, C_CONV), lambda bi, i: (0, 0)),
                  pl.BlockSpec((1, C_CONV), lambda bi, i: (0, 0)),
                  pl.BlockSpec((1, C_CONV), lambda bi, i: (0, 0))],
        out_specs=[pl.BlockSpec((1, tt, C_CONV), lambda bi, i: (bi, i, 0)),
                   pl.BlockSpec((1, hist, C_CONV), lambda bi, i: (bi, 0, 0))],
        out_shape=[jax.ShapeDtypeStruct((b, t, C_CONV), BF16),
                   jax.ShapeDtypeStruct((b, hist, C_CONV), F32)],
        scratch_shapes=[pltpu.VMEM((CONV_PAD + tt, C_CONV), F32),
                        pltpu.VMEM((tt, C_CONV), F32)],
        compiler_params=_params("parallel", "arbitrary"),
        name="conv_branch",
    )(h3, prev, conv_w, conv_b.reshape(1, C_CONV), ln_g.reshape(1, C_CONV), ln_b.reshape(1, C_CONV))


def _softmax_pv(s_a, s_b, v_a, v_b):
    m = jnp.maximum(jnp.max(s_a, axis=-1, keepdims=True), jnp.max(s_b, axis=-1, keepdims=True))
    p_a = jnp.exp(s_a - m)
    p_b = jnp.exp(s_b - m)
    l = jnp.sum(p_a, axis=-1, keepdims=True) + jnp.sum(p_b, axis=-1, keepdims=True)
    o = _dot(p_a.astype(BF16), v_a.astype(BF16)) + _dot(p_b.astype(BF16), v_b.astype(BF16))
    return o / l, m + jnp.log(l)


def _sw_prompt_kernel(slope_ref, q_ref, kc_ref, kp_ref, vc_ref, vp_ref, o_ref, lse_ref, *, r, nq):
    h = pl.program_id(1)
    n = pl.program_id(2)
    blk = SW_SPAN
    slope = slope_ref[h]
    ii = lax.broadcasted_iota(jnp.int32, (blk, blk), 0)
    cc = lax.broadcasted_iota(jnp.int32, (blk, blk), 1)
    back_c = ii - cc
    back_p = back_c + blk
    bias_c = jnp.where(back_c >= 0, -slope * (r * back_c).astype(F32), NEG_INF)
    bias_p = jnp.where(back_p <= SW_SPAN, -slope * (r * back_p).astype(F32), NEG_INF)
    bias_first = bias_p + jnp.where(n >= 1, 0.0, NEG_INF).astype(F32)

    def rows(rho, m):
        start = rho + r * blk * m
        return pl.ds(start, blk) if r == 1 else pl.ds(start, blk, stride=r)

    for rho in range(r):
        for m in range(nq):
            q = q_ref[0, rows(rho, m), :].astype(BF16)
            k_c = kc_ref[0, rows(rho, m), :]
            v_c = vc_ref[0, rows(rho, m), :]
            if m == 0:
                k_p = kp_ref[0, rows(rho, 0), :]
                v_p = vp_ref[0, rows(rho, 0), :]
                b_p = bias_first
            else:
                k_p = kc_ref[0, rows(rho, m - 1), :]
                v_p = vc_ref[0, rows(rho, m - 1), :]
                b_p = bias_p
            s_c = _dot_nt(q, k_c.astype(BF16)) + bias_c
            s_p = _dot_nt(q, k_p.astype(BF16)) + b_p
            o, lse = _softmax_pv(s_c, s_p, v_c, v_p)
            o_ref[0, rows(rho, m), :] = o
            lse_ref[0, rows(rho, m), :] = jnp.broadcast_to(lse, (blk, SW_HD))


def sw_prompt_group(h3, g):
    b, t, _ = h3.shape
    r = SW_DILATIONS[g]
    nq = max(1, 4 // r)
    pb = SW_SPAN * r
    ch = pb * nq
    assert t % ch == 0
    base = (COL_SW + 3 * g * SW_OUT) // SW_HD
    hpb = SW_OUT // SW_HD
    slopes = jnp.asarray(_alibi_slopes()[g])

    def cur(off):
        return pl.BlockSpec((1, ch, SW_HD), lambda bi, h, n: (bi, n, base + off * hpb + h))

    def prev(off):
        return pl.BlockSpec((1, pb, SW_HD), lambda bi, h, n: (bi, jnp.maximum(n * nq - 1, 0), base + off * hpb + h))

    out_spec = pl.BlockSpec((1, ch, SW_HD), lambda bi, h, n: (bi, n, h))
    out_sds = jax.ShapeDtypeStruct((b, t, SW_OUT), F32)
    return pl.pallas_call(
        functools.partial(_sw_prompt_kernel, r=r, nq=nq),
        grid=(b, SW_HG, t // ch),
        in_specs=[pl.BlockSpec(memory_space=pltpu.SMEM),
                  cur(0), cur(1), prev(1), cur(2), prev(2)],
        out_specs=[out_spec, out_spec],
        out_shape=[out_sds, out_sds],
        compiler_params=_params("parallel", "parallel", "arbitrary"),
        name=f"sw_prompt_g{g}",
    )(slopes, h3, h3, h3, h3, h3)


def _sw_sample_kernel(q_ref, kn_ref, vn_ref, kb_ref, vb_ref, o_ref, lse_ref, ko_ref, vo_ref,
                      *, r, buf_len, t_new, slopes):
    tq = lax.broadcasted_iota(jnp.int32, (t_new, buf_len), 0)
    pk = lax.broadcasted_iota(jnp.int32, (t_new, buf_len), 1)
    dist_b = buf_len + tq - pk
    ok_b = ((dist_b & (r - 1)) == 0) & (dist_b <= SW_SPAN * r)
    dist_bf = dist_b.astype(F32)
    tq2 = lax.broadcasted_iota(jnp.int32, (t_new, t_new), 0)
    tk2 = lax.broadcasted_iota(jnp.int32, (t_new, t_new), 1)
    dist_n = tq2 - tk2
    ok_n = (dist_n >= 0) & ((dist_n & (r - 1)) == 0)
    dist_nf = dist_n.astype(F32)
    keep = buf_len - t_new
    ko_ref[pl.ds(0, keep)] = kb_ref[pl.ds(t_new, keep)]
    vo_ref[pl.ds(0, keep)] = vb_ref[pl.ds(t_new, keep)]
    for h in range(SW_HG):
        sl = slice(h * SW_HD, (h + 1) * SW_HD)
        q = q_ref[0, :, sl].astype(BF16)
        k_n = kn_ref[0, :, sl]
        v_n = vn_ref[0, :, sl]
        ko_ref[pl.ds(keep, t_new), h, :] = k_n
        vo_ref[pl.ds(keep, t_new), h, :] = v_n
        s_b = _dot_nt(q, kb_ref[:, h, :].astype(BF16))
        s_n = _dot_nt(q, k_n.astype(BF16))
        s_b = jnp.where(ok_b, s_b - slopes[h] * dist_bf, NEG_INF)
        s_n = jnp.where(ok_n, s_n - slopes[h] * dist_nf, NEG_INF)
        o, lse = _softmax_pv(s_b, s_n, vb_ref[:, h, :], v_n)
        o_ref[0, :, sl] = o
        lse_ref[0, :, sl] = jnp.broadcast_to(lse, (t_new, SW_HD))


def sw_sample_group(h3, g, k_buf, v_buf):
    b, t, _ = h3.shape
    buf_len = k_buf.shape[1]
    r = SW_DILATIONS[g]
    assert buf_len >= SW_SPAN * r and t <= r * SW_SPAN and buf_len > t
    base = COL_SW // SW_OUT + 3 * g
    blk = (1, t, SW_OUT)
    bblk = (None, buf_len, SW_HG, SW_HD)
    slopes = tuple(float(s) for s in _alibi_slopes()[g])
    out_spec = pl.BlockSpec(blk, lambda bi: (bi, 0, 0))
    buf_spec = pl.BlockSpec(bblk, lambda bi: (bi, 0, 0, 0))
    out_sds = jax.ShapeDtypeStruct((b, t, SW_OUT), F32)
    buf_sds = jax.ShapeDtypeStruct(k_buf.shape, F32)
    return pl.pallas_call(
        functools.partial(_sw_sample_kernel, r=r, buf_len=buf_len, t_new=t, slopes=slopes),
        grid=(b,),
        in_specs=[pl.BlockSpec(blk, lambda bi: (bi, 0, base)),
                  pl.BlockSpec(blk, lambda bi: (bi, 0, base + 1)),
                  pl.BlockSpec(blk, lambda bi: (bi, 0, base + 2)),
                  buf_spec, buf_spec],
        out_specs=[out_spec, out_spec, buf_spec, buf_spec],
        out_shape=[out_sds, out_sds, buf_sds, buf_sds],
        compiler_params=_params("parallel"),
        name=f"sw_sample_g{g}",
    )(h3, h3, h3, k_buf, v_buf)


def _attend(q, k, v):
    s = _dot_nt(q.astype(BF16), k.astype(BF16))
    m = jnp.max(s, axis=-1, keepdims=True)
    p = jnp.exp(s - m)
    p = p / jnp.sum(p, axis=-1, keepdims=True)
    return _dot(p.astype(BF16), v.astype(BF16)).astype(BF16)


def _mem_attn_kernel(q_ref, k_ref, v_ref, o_ref):
    o_ref[0] = _attend(q_ref[0], k_ref[0], v_ref[0])


def mem_attention(h3, mem_k, mem_v):
    b, t, _ = h3.shape
    n_mem = mem_k.shape[1]
    tq = _pick(t, (512,))
    qbase = COL_MEMQ // MEM_HD
    return pl.pallas_call(
        _mem_attn_kernel,
        grid=(b, t // tq, MEM_H),
        in_specs=[pl.BlockSpec((1, tq, MEM_HD), lambda bi, i, h: (bi, i, qbase + h)),
                  pl.BlockSpec((1, n_mem, MEM_HD), lambda bi, i, h: (bi, 0, h)),
                  pl.BlockSpec((1, n_mem, MEM_HD), lambda bi, i, h: (bi, 0, h))],
        out_specs=pl.BlockSpec((1, tq, MEM_HD), lambda bi, i, h: (bi, i, h)),
        out_shape=jax.ShapeDtypeStruct((b, t, MEM_W), BF16),
        compiler_params=_params("parallel", "parallel", "arbitrary"),
        name="mem_attention",
    )(h3, mem_k, mem_v)


def _mem_attn_cache_kernel(qa_ref, qb_ref, k_ref, v_ref, o_ref):
    per = PROJ_TN // MEM_HD
    for h in range(MEM_H):
        q_ref = (qa_ref, qb_ref)[h // per]
        ql = slice((h % per) * MEM_HD, (h % per + 1) * MEM_HD)
        sl = slice(h * MEM_HD, (h + 1) * MEM_HD)
        o_ref[0, :, sl] = _attend(q_ref[0, :, ql], k_ref[:, h, :], v_ref[:, h, :])


def mem_attention_cache(h3, cache_k, cache_v):
    b, t, _ = h3.shape
    n_mem = cache_k.shape[1]
    assert MEM_W == 2 * PROJ_TN
    qbase = COL_MEMQ // PROJ_TN
    cspec = pl.BlockSpec((None, n_mem, MEM_H, MEM_HD), lambda bi: (bi, 0, 0, 0))
    return pl.pallas_call(
        _mem_attn_cache_kernel,
        grid=(b,),
        in_specs=[pl.BlockSpec((1, t, PROJ_TN), lambda bi: (bi, 0, qbase)),
                  pl.BlockSpec((1, t, PROJ_TN), lambda bi: (bi, 0, qbase + 1)), cspec, cspec],
        out_specs=pl.BlockSpec((1, t, MEM_W), lambda bi: (bi, 0, 0)),
        out_shape=jax.ShapeDtypeStruct((b, t, MEM_W), BF16),
        compiler_params=_params("parallel"),
        name="mem_attention_cache",
    )(h3, h3, cache_k, cache_v)


def _merge_kernel(ca_ref, o0_ref, o1_ref, o2_ref, l0_ref, l1_ref, l2_ref, mo_ref, g0_ref, g1_ref, g2_ref,
                  wc_ref, ws_ref, wm_ref, out_ref):
    l0, l1, l2 = l0_ref[...], l1_ref[...], l2_ref[...]
    m = jnp.maximum(jnp.maximum(l0, l1), l2)
    e0, e1, e2 = jnp.exp(l0 - m), jnp.exp(l1 - m), jnp.exp(l2 - m)
    sw = (e0 * o0_ref[...] + e1 * o1_ref[...] + e2 * o2_ref[...]) / (e0 + e1 + e2)
    conv_out = _dot(ca_ref[...], wc_ref[...])
    sw_out = _dot(sw.astype(BF16), ws_ref[...])
    mem_out = _dot(mo_ref[...], wm_ref[...])
    merged = (jax.nn.sigmoid(g0_ref[...]) * conv_out + jax.nn.sigmoid(g1_ref[...]) * sw_out
              + jax.nn.sigmoid(g2_ref[...]) * mem_out)
    out_ref[...] = merged.astype(BF16)


def merge_branches(hcat, c_act, sw_o, sw_lse, mem_o, w_conv_o, w_sw_o, w_mem_o):
    n = hcat.shape[0]
    tm = _pick(n, (512, 256))
    tn = 512
    gbase = COL_GATE // tn
    gstep = D_MODEL // tn

    def row(width):
        return pl.BlockSpec((tm, width), lambda i, j: (i, 0))

    def gate(br):
        return pl.BlockSpec((tm, tn), lambda i, j: (i, gbase + br * gstep + j))

    def wcol(k):
        return pl.BlockSpec((k, tn), lambda i, j: (0, j))

    return pl.pallas_call(
        _merge_kernel,
        grid=(n // tm, D_MODEL // tn),
        in_specs=[row(C_CONV), row(SW_OUT), row(SW_OUT), row(SW_OUT), row(SW_OUT), row(SW_OUT), row(SW_OUT),
                  row(MEM_W), gate(0), gate(1), gate(2), wcol(C_CONV), wcol(SW_OUT), wcol(MEM_W)],
        out_specs=pl.BlockSpec((tm, tn), lambda i, j: (i, j)),
        out_shape=jax.ShapeDtypeStruct((n, D_MODEL), BF16),
        compiler_params=_params("parallel", "arbitrary"),
        name="merge_branches",
    )(c_act, *sw_o, *sw_lse, mem_o, hcat, hcat, hcat, w_conv_o, w_sw_o, w_mem_o)


def _top_values(s, rows16, k):
    vals = jnp.full(rows16.shape, NEG_INF, F32)
    work = s
    for i in range(k):
        mx = jnp.max(work, axis=0, keepdims=True)
        vals = jnp.where(rows16 == i, mx, vals)
        work = jnp.where(work == mx, NEG_INF, work)
    return vals


def _router_kernel(q_ref, k1_ref, k2_ref, s1_ref, s2_ref, e1_ref, e2_ref, tau_ref, *, tm):
    half = PEER_DKEY // 2
    rows16 = lax.broadcasted_iota(jnp.int32, (PEER_TOPK, tm), 0)
    rows8 = lax.broadcasted_iota(jnp.int32, (SUBLANES, tm), 0)
    k1 = k1_ref[...]
    k2 = k2_ref[...]

    def head(h, carry):
        c0 = pl.multiple_of(h * PEER_DKEY, PEER_DKEY)
        qa = q_ref[:, pl.ds(c0, half)].astype(BF16)
        qb = q_ref[:, pl.ds(c0 + half, half)].astype(BF16)
        s1 = _dot_nt(k1, qa)
        s2 = _dot_nt(k2, qb)
        s1 = s1 - jnp.max(s1, axis=0, keepdims=True)
        s2 = s2 - jnp.max(s2, axis=0, keepdims=True)
        v1 = _top_values(s1, rows16, PEER_TOPK)
        v2 = _top_values(s2, rows16, PEER_TOPK)
        parts = [v1[0:1] + v2]
        v2_top = v2[0:SUBLANES]
        for i in range(1, PEER_TOPK):
            keep = PEER_TOPK // (i + 1)
            parts.append(jnp.where(rows8 < keep, v1[i:i + 1] + v2_top, NEG_INF))
        cand = jnp.concatenate(parts, axis=0)
        work = cand
        tau = None
        for _i in range(PEER_TOPK):
            tau = jnp.max(work, axis=0, keepdims=True)
            work = jnp.where(work == tau, NEG_INF, work)
        z = jnp.sum(jnp.where(cand >= tau, jnp.exp(cand), 0.0), axis=0, keepdims=True)
        s1_ref[h] = s1
        s2_ref[h] = s2
        e1_ref[h] = jnp.exp(s1) * (1.0 / z)
        e2_ref[h] = jnp.exp(s2)
        tau_ref[pl.ds(h, 1), :] = tau
        return carry

    lax.fori_loop(0, PEER_HEADS, head, 0)


def peer_router(q, k1_bf, k2_bf):
    n = q.shape[0]
    tm = 256
    big = pl.BlockSpec((PEER_HEADS, PEER_NKEYS, tm), lambda i: (0, 0, i))
    big_sds = jax.ShapeDtypeStruct((PEER_HEADS, PEER_NKEYS, n), F32)
    return pl.pallas_call(
        functools.partial(_router_kernel, tm=tm),
        grid=(n // tm,),
        in_specs=[pl.BlockSpec((tm, PEER_HEADS * PEER_DKEY), lambda i: (i, 0)),
                  pl.BlockSpec((PEER_NKEYS, PEER_DKEY // 2), lambda i: (0, 0)),
                  pl.BlockSpec((PEER_NKEYS, PEER_DKEY // 2), lambda i: (0, 0))],
        out_specs=[big, big, big, big, pl.BlockSpec((PEER_HEADS, tm), lambda i: (0, i))],
        out_shape=[big_sds, big_sds, big_sds, big_sds, jax.ShapeDtypeStruct((PEER_HEADS, n), F32)],
        compiler_params=_params("parallel"),
        name="peer_router",
    )(q, k1_bf, k2_bf)


def _gelu_tanh(x):
    return 0.5 * x * (1.0 + jnp.tanh(math.sqrt(2.0 / math.pi) * (x + 0.044715 * (x * x * x))))


def _peer_kernel(h_ref, g2_ref, s1_ref, e1_ref, s2_ref, e2_ref, tau_ref, u_ref, vt_ref, y_ref,
                 hn_ref, acc_ref, act_new_ref, act_old_ref, coef_new_ref, coef_old_ref, *, tm, te, nj):
    s = pl.program_id(1)
    rows = te // PEER_NKEYS
    pack = 2 * SUBLANES

    @pl.when(s == 0)
    def _():
        hn_ref[...] = _rms(h_ref[...], g2_ref[...]).T.astype(BF16)
        acc_ref[...] = jnp.zeros_like(acc_ref)
        act_old_ref[...] = jnp.zeros_like(act_old_ref)
        coef_old_ref[...] = jnp.zeros_like(coef_old_ref)

    upper = (jnp.clip(s - 1, 0, nj - 1) % 2) == 1

    def row_of(ref, h, al, cs):
        picked = jnp.where(upper, ref[h, rows + al:rows + al + 1, cs], ref[h, al:al + 1, cs])
        return jnp.broadcast_to(picked, (pack, LANES))

    def score(ts):
        act_new_ref[:, ts] = _dot(u_ref[...], hn_ref[:, ts])

    def accumulate(ts):
        acc_ref[:, ts] += _dot(vt_ref[...], coef_old_ref[:, ts])

    def gate_tile(al, c):
        cs = slice(c * LANES, (c + 1) * LANES)
        s1c, e1c, tau = [], [], []
        for h in range(PEER_HEADS):
            s1c.append(row_of(s1_ref, h, al, cs))
            e1c.append(row_of(e1_ref, h, al, cs))
            tau.append(jnp.broadcast_to(tau_ref[h:h + 1, cs], (pack, LANES)))
        for sb in range(PEER_NKEYS // pack):
            bs = slice(sb * pack, (sb + 1) * pack)
            gate = jnp.zeros((pack, LANES), F32)
            for h in range(PEER_HEADS):
                total = s1c[h] + s2_ref[h, bs, cs]
                gate = gate + jnp.where(total >= tau[h], e2_ref[h, bs, cs] * e1c[h], 0.0)
            er = slice(al * PEER_NKEYS + sb * pack, al * PEER_NKEYS + (sb + 1) * pack)
            coef_new_ref[er, cs] = (gate * _gelu_tanh(act_old_ref[er, cs])).astype(BF16)

    halves = [slice(t * (tm // 2), (t + 1) * (tm // 2)) for t in range(2)]
    matmuls = [functools.partial(score, halves[0]), functools.partial(accumulate, halves[0]),
               functools.partial(score, halves[1]), functools.partial(accumulate, halves[1])]
    tiles = [(al, c) for al in range(rows) for c in range(tm // LANES)]
    per = -(-len(tiles) // len(matmuls))
    for k, mm in enumerate(matmuls):
        mm()
        for al, c in tiles[k * per:(k + 1) * per]:
            gate_tile(al, c)
    act_old_ref[...] = act_new_ref[...]
    coef_old_ref[...] = coef_new_ref[...]

    @pl.when(s == pl.num_programs(1) - 1)
    def _():
        y_ref[...] = h_ref[...] + acc_ref[...].T


def peer_experts(h, norm2_g, routed, u_bf, vt_bf):
    n, d = h.shape
    n_exp = u_bf.shape[0]
    tm = _pick(n, (512, 256))
    te = 512
    nj = n_exp // te
    rows = te // PEER_NKEYS
    assert 2 * rows == SUBLANES
    s1, s2, e1, e2, tau = routed

    def blk(off):
        return lambda i, s: jnp.clip(s - off, 0, nj - 1)

    row = pl.BlockSpec((PEER_HEADS, SUBLANES, tm), lambda i, s: (0, blk(1)(i, s) * rows // SUBLANES, i))
    big = pl.BlockSpec((PEER_HEADS, PEER_NKEYS, tm), lambda i, s: (0, 0, i))
    return pl.pallas_call(
        functools.partial(_peer_kernel, tm=tm, te=te, nj=nj),
        grid=(n // tm, nj + 2),
        in_specs=[pl.BlockSpec((tm, d), lambda i, s: (i, 0)),
                  pl.BlockSpec((1, d), lambda i, s: (0, 0)),
                  row, row, big, big,
                  pl.BlockSpec((PEER_HEADS, tm), lambda i, s: (0, i)),
                  pl.BlockSpec((te, d), lambda i, s: (blk(0)(i, s), 0)),
                  pl.BlockSpec((d, te), lambda i, s: (0, blk(2)(i, s)))],
        out_specs=pl.BlockSpec((tm, d), lambda i, s: (i, 0)),
        out_shape=jax.ShapeDtypeStruct((n, d), F32),
        scratch_shapes=[pltpu.VMEM((d, tm), BF16),
                        pltpu.VMEM((d, tm), F32),
                        pltpu.VMEM((te, tm), F32), pltpu.VMEM((te, tm), F32),
                        pltpu.VMEM((te, tm), BF16), pltpu.VMEM((te, tm), BF16)],
        compiler_params=_params("parallel", "arbitrary"),
        name="peer_experts",
    )(h, norm2_g.reshape(1, d), s1, e1, s2, e2, tau, u_bf, vt_bf)


def _in_proj_epilogue(sw_q_g, sw_k_g, mem_q_g):
    modes = np.zeros((IN_COLS // PROJ_TN,), np.int32)
    gain = [jnp.ones((IN_CONV,), F32)]
    for g in range(SW_GROUPS):
        t0 = (COL_SW + 3 * g * SW_OUT) // PROJ_TN
        modes[t0:t0 + 2 * SW_OUT // PROJ_TN] = TILE_NORM128
        gain += [jnp.tile(sw_q_g[g], SW_HG) * (SW_HD ** -0.5), jnp.tile(sw_k_g[g], SW_HG), jnp.ones((SW_OUT,), F32)]
    modes[COL_MEMQ // PROJ_TN:COL_GATE // PROJ_TN] = TILE_NORM256
    gain += [jnp.tile(mem_q_g, MEM_H) * (MEM_HD ** -0.5), jnp.ones((3 * D_MODEL,), F32)]
    return modes, jnp.concatenate(gain)


def _layer(x, conv_prev, sw_bufs, mem_k, mem_v, w):
    b, t, d = x.shape
    n = b * t
    x2 = x.reshape(n, d)
    hcat = norm_matmul(x2, w["norm1_g"], w["w_in"], *w["in_epilogue"])
    h3 = hcat.reshape(b, t, IN_COLS)
    c_act, conv_state = conv_branch(h3, conv_prev, w["conv_w"], w["conv_b"], w["conv_ln_g"], w["conv_ln_b"])
    sw_o, sw_lse, states = [], [], []
    for g in range(SW_GROUPS):
        if sw_bufs is None:
            o, lse = sw_prompt_group(h3, g)
            keep = min(SW_WINDOWS[g], t)
            for off in (1, 2):
                col = COL_SW + (3 * g + off) * SW_OUT
                states.append(h3[:, t - keep:, col:col + SW_OUT].reshape(b, keep, SW_HG, SW_HD))
        else:
            o, lse, k_new, v_new = sw_sample_group(h3, g, sw_bufs[2 * g], sw_bufs[2 * g + 1])
            states += [k_new, v_new]
        sw_o.append(o.reshape(n, SW_OUT))
        sw_lse.append(lse.reshape(n, SW_OUT))
    if sw_bufs is None:
        mem_o = mem_attention(h3, mem_k, mem_v)
    else:
        mem_o = mem_attention_cache(h3, mem_k, mem_v)
    merged = merge_branches(hcat, c_act.reshape(n, C_CONV), sw_o, sw_lse, mem_o.reshape(n, MEM_W),
                            w["w_conv_o"], w["w_sw_o"], w["w_mem_o"])
    hres = matmul_residual(merged, w["w_out"], x2)
    q = norm_matmul(hres, w["norm2_g"], w["peer_w_q"])
    routed = peer_router(q, w["peer_k1"], w["peer_k2"])
    y = peer_experts(hres, w["norm2_g"], routed, w["peer_u"], w["peer_vt"])
    return y.reshape(b, t, d), conv_state, states


def kernel(x_prompt, x_sample, mem_prompt, state_conv, state_sw0_k, state_sw0_v, state_sw1_k, state_sw1_v, state_sw2_k, state_sw2_v, cache_mem_k, cache_mem_v, norm1_g, mem_norm_g, w_in, conv_w, conv_b, conv_ln_g, conv_ln_b, w_conv_o, sw_q_g, sw_k_g, w_sw_o, w_mem_k, w_mem_v, mem_q_g, mem_k_g, w_mem_o, w_out, norm2_g, peer_w_q, peer_k1, peer_k2, peer_u, peer_v):
    depth = w_in.shape[0]
    bp, tp, _ = x_prompt.shape
    n_mem = mem_prompt.shape[1]
    sw_states = (state_sw0_k, state_sw0_v, state_sw1_k, state_sw1_v, state_sw2_k, state_sw2_v)
    yp, ys = x_prompt, x_sample
    conv_p, conv_s, memk_p, memv_p = [], [], [], []
    sw_p = [[] for _ in range(2 * SW_GROUPS)]
    sw_s = [[] for _ in range(2 * SW_GROUPS)]
    for l in range(depth):
        w = dict(
            norm1_g=norm1_g[l], w_in=w_in[l].astype(BF16), conv_w=conv_w[l], conv_b=conv_b[l],
            conv_ln_g=conv_ln_g[l], conv_ln_b=conv_ln_b[l], w_conv_o=w_conv_o[l].astype(BF16),
            w_sw_o=w_sw_o[l].astype(BF16), w_mem_o=w_mem_o[l].astype(BF16), w_out=w_out[l].astype(BF16),
            norm2_g=norm2_g[l], peer_w_q=peer_w_q[l].astype(BF16), peer_k1=peer_k1[l].astype(BF16),
            peer_k2=peer_k2[l].astype(BF16), peer_u=peer_u[l].astype(BF16), peer_vt=peer_v[l].T.astype(BF16),
            in_epilogue=_in_proj_epilogue(sw_q_g[l], sw_k_g[l], mem_q_g[l]))
        w_kv = jnp.concatenate([w_mem_k[l], w_mem_v[l]], axis=1).astype(BF16)
        kv_modes = np.array([TILE_NORM256] * (MEM_W // PROJ_TN) + [TILE_PLAIN] * (MEM_W // PROJ_TN), np.int32)
        kv_gain = jnp.concatenate([jnp.tile(mem_k_g[l], MEM_H), jnp.ones((MEM_W,), F32)])
        kv = norm_matmul(mem_prompt.reshape(bp * n_mem, D_MODEL), mem_norm_g[l], w_kv, kv_modes, kv_gain)
        mk = kv[:, :MEM_W].reshape(bp, n_mem, MEM_W)
        mv = kv[:, MEM_W:].reshape(bp, n_mem, MEM_W)

        zeros_prev = jnp.zeros((bp, CONV_W - 1, C_CONV), F32)
        yp, cp, swp_l = _layer(yp, zeros_prev, None, mk, mv, w)
        ys, cs, sws_l = _layer(ys, state_conv[l], [s[l] for s in sw_states], cache_mem_k[l], cache_mem_v[l], w)
        conv_p.append(cp)
        conv_s.append(cs)
        memk_p.append(mk.reshape(bp, n_mem, MEM_H, MEM_HD))
        memv_p.append(mv.reshape(bp, n_mem, MEM_H, MEM_HD))
        for i in range(2 * SW_GROUPS):
            sw_p[i].append(swp_l[i])
            sw_s[i].append(sws_l[i])
    swp = [jnp.stack(a, axis=0) for a in sw_p]
    sws = [jnp.stack(a, axis=0) for a in sw_s]
    return (yp, ys, jnp.stack(conv_p, axis=0), jnp.stack(conv_s, axis=0),
            swp[0], swp[1], swp[2], swp[3], swp[4], swp[5],
            sws[0], sws[1], sws[2], sws[3], sws[4], sws[5],
            jnp.stack(memk_p, axis=0), jnp.stack(memv_p, axis=0))
```

```python
import functools
import math

import numpy as np
import jax
import jax.numpy as jnp
from jax import lax
from jax.experimental import pallas as pl
from jax.experimental.pallas import tpu as pltpu

F32 = jnp.float32
BF16 = jnp.bfloat16
EPS = 1e-6
NEG_INF = float("-inf")

LANES = 128
SUBLANES = 8
VMEM_LIMIT = 60 * 1024 * 1024

D_MODEL = 2048
C_CONV = 1024
CONV_W = 31
CONV_PAD = 32
SW_GROUPS = 3
SW_DILATIONS = (1, 4, 16)
SW_WINDOWS = (128, 512, 2048)
SW_SPAN = 128
SW_HG = 4
SW_HD = 128
SW_OUT = SW_HG * SW_HD
MEM_H = 4
MEM_HD = 256
MEM_W = MEM_H * MEM_HD
IN_CONV = 2 * C_CONV
IN_SW = SW_GROUPS * 3 * SW_OUT
IN_COLS = IN_CONV + IN_SW + MEM_W + 3 * D_MODEL
COL_SW = IN_CONV
COL_MEMQ = IN_CONV + IN_SW
COL_GATE = COL_MEMQ + MEM_W
PROJ_TN = 512
PEER_HEADS = 8
PEER_NKEYS = 128
PEER_DKEY = 256
PEER_TOPK = 16
PEER_TCHUNK = 256

TILE_PLAIN, TILE_NORM128, TILE_NORM256 = 0, 1, 2
_TILE_HD = {TILE_NORM128: SW_HD, TILE_NORM256: MEM_HD}


def _alibi_slopes():
    n = SW_GROUPS * SW_HG
    s = np.float32(2.0) ** (np.float32(-8.0) * np.arange(1, n + 1, dtype=np.float32) / np.float32(n))
    return s.reshape(SW_GROUPS, SW_HG)


def _rms(x, g):
    return x * lax.rsqrt(jnp.mean(x * x, axis=-1, keepdims=True) + EPS) * g


def _dot(a, b):
    return jnp.dot(a, b, preferred_element_type=F32)


def _dot_nt(a, b):
    return lax.dot_general(a, b, (((1,), (1,)), ((), ())), preferred_element_type=F32)


def _params(*sem):
    return pltpu.CompilerParams(dimension_semantics=sem, vmem_limit_bytes=VMEM_LIMIT)


def _pick(n, options):
    for o in options:
        if n % o == 0:
            return o
    raise ValueError(f"no tile in {options} divides {n}")


def _norm_matmul_kernel(mode_ref, x_ref, g_ref, w_ref, cg_ref, o_ref, xn_ref, *, tn, head_modes):
    j = pl.program_id(1)

    @pl.when(j == 0)
    def _():
        xn_ref[...] = _rms(x_ref[...], g_ref[...]).astype(BF16)

    acc = _dot(xn_ref[...], w_ref[...])
    if not head_modes:
        o_ref[...] = acc
        return
    mode = mode_ref[j]

    @pl.when(mode == TILE_PLAIN)
    def _():
        o_ref[...] = acc

    for code in head_modes:
        hd = _TILE_HD[code]

        @pl.when(mode == code)
        def _():
            for h in range(tn // hd):
                sl = slice(h * hd, (h + 1) * hd)
                o_ref[:, sl] = _rms(acc[:, sl], cg_ref[:, sl])


def norm_matmul(x, g, w_bf, tile_modes=None, col_gain=None):
    n, d = x.shape
    cols = w_bf.shape[1]
    tm = _pick(n, (1024, 256))
    tn = PROJ_TN
    nt = cols // tn
    if tile_modes is None:
        tile_modes = np.zeros((nt,), np.int32)
        col_gain = jnp.ones((cols,), F32)
    head_modes = tuple(sorted(set(int(m) for m in tile_modes) - {TILE_PLAIN}))
    grid_spec = pltpu.PrefetchScalarGridSpec(
        num_scalar_prefetch=1,
        grid=(n // tm, nt),
        in_specs=[pl.BlockSpec((tm, d), lambda i, j, m: (i, 0)),
                  pl.BlockSpec((1, d), lambda i, j, m: (0, 0)),
                  pl.BlockSpec((d, tn), lambda i, j, m: (0, j)),
                  pl.BlockSpec((1, tn), lambda i, j, m: (0, j))],
        out_specs=pl.BlockSpec((tm, tn), lambda i, j, m: (i, j)),
        scratch_shapes=[pltpu.VMEM((tm, d), BF16)])
    return pl.pallas_call(
        functools.partial(_norm_matmul_kernel, tn=tn, head_modes=head_modes),
        grid_spec=grid_spec,
        out_shape=jax.ShapeDtypeStruct((n, cols), F32),
        compiler_params=_params("parallel", "arbitrary"),
        name="norm_matmul",
    )(jnp.asarray(tile_modes, jnp.int32), x, g.reshape(1, d), w_bf, col_gain.reshape(1, cols))


def _matmul_res_kernel(a_ref, w_ref, x_ref, o_ref):
    o_ref[...] = x_ref[...] + _dot(a_ref[...], w_ref[...])


def matmul_residual(a_bf, w_bf, x):
    n, k = a_bf.shape
    cols = w_bf.shape[1]
    tm = _pick(n, (512, 256))
    tn = _pick(cols, (1024,))
    return pl.pallas_call(
        _matmul_res_kernel,
        grid=(n // tm, cols // tn),
        in_specs=[pl.BlockSpec((tm, k), lambda i, j: (i, 0)),
                  pl.BlockSpec((k, tn), lambda i, j: (0, j)),
                  pl.BlockSpec((tm, tn), lambda i, j: (i, j))],
        out_specs=pl.BlockSpec((tm, tn), lambda i, j: (i, j)),
        out_shape=jax.ShapeDtypeStruct((n, cols), F32),
        compiler_params=_params("parallel", "arbitrary"),
        name="matmul_residual",
    )(a_bf, w_bf, x)


def _conv_kernel(hc_ref, prev_ref, cw_ref, cb_ref, lg_ref, lb_ref, act_ref, st_ref, ext_ref, c_ref, *, tt):
    i = pl.program_id(1)
    hist = CONV_W - 1
    lo = CONV_PAD - hist

    @pl.when(i == 0)
    def _():
        ext_ref[lo:CONV_PAD, :] = prev_ref[0]

    @pl.when(i > 0)
    def _():
        ext_ref[lo:CONV_PAD, :] = ext_ref[tt + lo:tt + CONV_PAD, :]

    a = hc_ref[0, :, :C_CONV]
    b = hc_ref[0, :, C_CONV:]
    ext_ref[CONV_PAD:CONV_PAD + tt, :] = a * jax.nn.sigmoid(b)

    for c in range(C_CONV // LANES):
        cs = slice(c * LANES, (c + 1) * LANES)
        acc = jnp.zeros((tt, LANES), F32)
        for w in range(CONV_W):
            acc = acc + ext_ref[lo + w:lo + w + tt, cs] * cw_ref[w:w + 1, cs]
        c_ref[:, cs] = acc + cb_ref[:, cs]

    cv = c_ref[...]
    mu = jnp.mean(cv, axis=-1, keepdims=True)
    xc = cv - mu
    var = jnp.mean(xc * xc, axis=-1, keepdims=True)
    y = xc * lax.rsqrt(var + EPS) * lg_ref[...] + lb_ref[...]
    act_ref[0] = (y * jax.nn.sigmoid(y)).astype(BF16)

    @pl.when(i == pl.num_programs(1) - 1)
    def _():
        st_ref[0] = ext_ref[tt + lo:tt + CONV_PAD, :]


def conv_branch(h3, prev, conv_w, conv_b, ln_g, ln_b):
    b, t, _ = h3.shape
    tt = _pick(t, (128, 8))
    hist = CONV_W - 1
    return pl.pallas_call(
        functools.partial(_conv_kernel, tt=tt),
        grid=(b, t // tt),
        in_specs=[pl.BlockSpec((1, tt, IN_CONV), lambda bi, i: (bi, i, 0)),
                  pl.BlockSpec((1, hist, C_CONV), lambda bi, i: (bi, 0, 0)),
                  pl.BlockSpec((CONV_W, C_CONV), lambda bi, i: (0, 0)),
                  pl.BlockSpec((1, C_CONV), lambda bi, i: (0, 0)),
                  pl.BlockSpec((1, C_CONV), lambda bi, i: (0, 0)),
                  pl.BlockSpec((1, C_CONV), lambda bi, i: (0, 0))],
        out_specs=[pl.BlockSpec((1, tt, C_CONV), lambda bi, i: (bi, i, 0)),
                   pl.BlockSpec((1, hist, C_CONV), lambda bi, i: (bi, 0, 0))],
        out_shape=[jax.ShapeDtypeStruct((b, t, C_CONV), BF16),
                   jax.ShapeDtypeStruct((b, hist, C_CONV), F32)],
        scratch_shapes=[pltpu.VMEM((CONV_PAD + tt, C_CONV), F32),
                        pltpu.VMEM((tt, C_CONV), F32)],
        compiler_params=_params("parallel", "arbitrary"),
        name="conv_branch",
    )(h3, prev, conv_w, conv_b.reshape(1, C_CONV), ln_g.reshape(1, C_CONV), ln_b.reshape(1, C_CONV))


def _softmax_pv(s_a, s_b, v_a, v_b):
    m = jnp.maximum(jnp.max(s_a, axis=-1, keepdims=True), jnp.max(s_b, axis=-1, keepdims=True))
    p_a = jnp.exp(s_a - m)
    p_b = jnp.exp(s_b - m)
    l = jnp.sum(p_a, axis=-1, keepdims=True) + jnp.sum(p_b, axis=-1, keepdims=True)
    o = _dot(p_a.astype(BF16), v_a.astype(BF16)) + _dot(p_b.astype(BF16), v_b.astype(BF16))
    return o / l, m + jnp.log(l)


def _sw_prompt_kernel(slope_ref, q_ref, kc_ref, kp_ref, vc_ref, vp_ref, o_ref, lse_ref, *, r, nq):
    h = pl.program_id(1)
    n = pl.program_id(2)
    blk = SW_SPAN
    slope = slope_ref[h]
    ii = lax.broadcasted_iota(jnp.int32, (blk, blk), 0)
    cc = lax.broadcasted_iota(jnp.int32, (blk, blk), 1)
    back_c = ii - cc
    back_p = back_c + blk
    bias_c = jnp.where(back_c >= 0, -slope * (r * back_c).astype(F32), NEG_INF)
    bias_p = jnp.where(back_p <= SW_SPAN, -slope * (r * back_p).astype(F32), NEG_INF)
    bias_first = bias_p + jnp.where(n >= 1, 0.0, NEG_INF).astype(F32)

    def rows(rho, m):
        start = rho + r * blk * m
        return pl.ds(start, blk) if r == 1 else pl.ds(start, blk, stride=r)

    for rho in range(r):
        for m in range(nq):
            q = q_ref[0, rows(rho, m), :].astype(BF16)
            k_c = kc_ref[0, rows(rho, m), :]
            v_c = vc_ref[0, rows(rho, m), :]
            if m == 0:
                k_p = kp_ref[0, rows(rho, 0), :]
                v_p = vp_ref[0, rows(rho, 0), :]
                b_p = bias_first
            else:
                k_p = kc_ref[0, rows(rho, m - 1), :]
                v_p = vc_ref[0, rows(rho, m - 1), :]
                b_p = bias_p
            s_c = _dot_nt(q, k_c.astype(BF16)) + bias_c
            s_p = _dot_nt(q, k_p.astype(BF16)) + b_p
            o, lse = _softmax_pv(s_c, s_p, v_c, v_p)
            o_ref[0, rows(rho, m), :] = o
            lse_ref[0, rows(rho, m), :] = jnp.broadcast_to(lse, (blk, SW_HD))


def sw_prompt_group(h3, g):
    b, t, _ = h3.shape
    r = SW_DILATIONS[g]
    nq = max(1, 4 // r)
    pb = SW_SPAN * r
    ch = pb * nq
    assert t % ch == 0
    base = (COL_SW + 3 * g * SW_OUT) // SW_HD
    hpb = SW_OUT // SW_HD
    slopes = jnp.asarray(_alibi_slopes()[g])

    def cur(off):
        return pl.BlockSpec((1, ch, SW_HD), lambda bi, h, n: (bi, n, base + off * hpb + h))

    def prev(off):
        return pl.BlockSpec((1, pb, SW_HD), lambda bi, h, n: (bi, jnp.maximum(n * nq - 1, 0), base + off * hpb + h))

    out_spec = pl.BlockSpec((1, ch, SW_HD), lambda bi, h, n: (bi, n, h))
    out_sds = jax.ShapeDtypeStruct((b, t, SW_OUT), F32)
    return pl.pallas_call(
        functools.partial(_sw_prompt_kernel, r=r, nq=nq),
        grid=(b, SW_HG, t // ch),
        in_specs=[pl.BlockSpec(memory_space=pltpu.SMEM),
                  cur(0), cur(1), prev(1), cur(2), prev(2)],
        out_specs=[out_spec, out_spec],
        out_shape=[out_sds, out_sds],
        compiler_params=_params("parallel", "parallel", "arbitrary"),
        name=f"sw_prompt_g{g}",
    )(slopes, h3, h3, h3, h3, h3)


def _sw_sample_kernel(q_ref, kn_ref, vn_ref, kb_ref, vb_ref, o_ref, lse_ref, ko_ref, vo_ref,
                      *, r, buf_len, t_new, slopes):
    tq = lax.broadcasted_iota(jnp.int32, (t_new, buf_len), 0)
    pk = lax.broadcasted_iota(jnp.int32, (t_new, buf_len), 1)
    dist_b = buf_len + tq - pk
    ok_b = ((dist_b & (r - 1)) == 0) & (dist_b <= SW_SPAN * r)
    dist_bf = dist_b.astype(F32)
    tq2 = lax.broadcasted_iota(jnp.int32, (t_new, t_new), 0)
    tk2 = lax.broadcasted_iota(jnp.int32, (t_new, t_new), 1)
    dist_n = tq2 - tk2
    ok_n = (dist_n >= 0) & ((dist_n & (r - 1)) == 0)
    dist_nf = dist_n.astype(F32)
    keep = buf_len - t_new
    ko_ref[pl.ds(0, keep)] = kb_ref[pl.ds(t_new, keep)]
    vo_ref[pl.ds(0, keep)] = vb_ref[pl.ds(t_new, keep)]
    for h in range(SW_HG):
        sl = slice(h * SW_HD, (h + 1) * SW_HD)
        q = q_ref[0, :, sl].astype(BF16)
        k_n = kn_ref[0, :, sl]
        v_n = vn_ref[0, :, sl]
        ko_ref[pl.ds(keep, t_new), h, :] = k_n
        vo_ref[pl.ds(keep, t_new), h, :] = v_n
        s_b = _dot_nt(q, kb_ref[:, h, :].astype(BF16))
        s_n = _dot_nt(q, k_n.astype(BF16))
        s_b = jnp.where(ok_b, s_b - slopes[h] * dist_bf, NEG_INF)
        s_n = jnp.where(ok_n, s_n - slopes[h] * dist_nf, NEG_INF)
        o, lse = _softmax_pv(s_b, s_n, vb_ref[:, h, :], v_n)
        o_ref[0, :, sl] = o
        lse_ref[0, :, sl] = jnp.broadcast_to(lse, (t_new, SW_HD))


def sw_sample_group(h3, g, k_buf, v_buf):
    b, t, _ = h3.shape
    buf_len = k_buf.shape[1]
    r = SW_DILATIONS[g]
    assert buf_len >= SW_SPAN * r and t <= r * SW_SPAN and buf_len > t
    base = COL_SW // SW_OUT + 3 * g
    blk = (1, t, SW_OUT)
    bblk = (None, buf_len, SW_HG, SW_HD)
    slopes = tuple(float(s) for s in _alibi_slopes()[g])
    out_spec = pl.BlockSpec(blk, lambda bi: (bi, 0, 0))
    buf_spec = pl.BlockSpec(bblk, lambda bi: (bi, 0, 0, 0))
    out_sds = jax.ShapeDtypeStruct((b, t, SW_OUT), F32)
    buf_sds = jax.ShapeDtypeStruct(k_buf.shape, F32)
    return pl.pallas_call(
        functools.partial(_sw_sample_kernel, r=r, buf_len=buf_len, t_new=t, slopes=slopes),
        grid=(b,),
        in_specs=[pl.BlockSpec(blk, lambda bi: (bi, 0, base)),
                  pl.BlockSpec(blk, lambda bi: (bi, 0, base + 1)),
                  pl.BlockSpec(blk, lambda bi: (bi, 0, base + 2)),
                  buf_spec, buf_spec],
        out_specs=[out_spec, out_spec, buf_spec, buf_spec],
        out_shape=[out_sds, out_sds, buf_sds, buf_sds],
        compiler_params=_params("parallel"),
        name=f"sw_sample_g{g}",
    )(h3, h3, h3, k_buf, v_buf)


def _attend(q, k, v):
    s = _dot_nt(q.astype(BF16), k.astype(BF16))
    m = jnp.max(s, axis=-1, keepdims=True)
    p = jnp.exp(s - m)
    p = p / jnp.sum(p, axis=-1, keepdims=True)
    return _dot(p.astype(BF16), v.astype(BF16)).astype(BF16)


def _mem_attn_kernel(q_ref, k_ref, v_ref, o_ref):
    o_ref[0] = _attend(q_ref[0], k_ref[0], v_ref[0])


def mem_attention(h3, mem_k, mem_v):
    b, t, _ = h3.shape
    n_mem = mem_k.shape[1]
    tq = _pick(t, (512,))
    qbase = COL_MEMQ // MEM_HD
    return pl.pallas_call(
        _mem_attn_kernel,
        grid=(b, t // tq, MEM_H),
        in_specs=[pl.BlockSpec((1, tq, MEM_HD), lambda bi, i, h: (bi, i, qbase + h)),
                  pl.BlockSpec((1, n_mem, MEM_HD), lambda bi, i, h: (bi, 0, h)),
                  pl.BlockSpec((1, n_mem, MEM_HD), lambda bi, i, h: (bi, 0, h))],
        out_specs=pl.BlockSpec((1, tq, MEM_HD), lambda bi, i, h: (bi, i, h)),
        out_shape=jax.ShapeDtypeStruct((b, t, MEM_W), BF16),
        compiler_params=_params("parallel", "parallel", "arbitrary"),
        name="mem_attention",
    )(h3, mem_k, mem_v)


def _mem_attn_cache_kernel(qa_ref, qb_ref, k_ref, v_ref, o_ref):
    per = PROJ_TN // MEM_HD
    for h in range(MEM_H):
        q_ref = (qa_ref, qb_ref)[h // per]
        ql = slice((h % per) * MEM_HD, (h % per + 1) * MEM_HD)
        sl = slice(h * MEM_HD, (h + 1) * MEM_HD)
        o_ref[0, :, sl] = _attend(q_ref[0, :, ql], k_ref[:, h, :], v_ref[:, h, :])


def mem_attention_cache(h3, cache_k, cache_v):
    b, t, _ = h3.shape
    n_mem = cache_k.shape[1]
    assert MEM_W == 2 * PROJ_TN
    qbase = COL_MEMQ // PROJ_TN
    cspec = pl.BlockSpec((None, n_mem, MEM_H, MEM_HD), lambda bi: (bi, 0, 0, 0))
    return pl.pallas_call(
        _mem_attn_cache_kernel,
        grid=(b,),
        in_specs=[pl.BlockSpec((1, t, PROJ_TN), lambda bi: (bi, 0, qbase)),
                  pl.BlockSpec((1, t, PROJ_TN), lambda bi: (bi, 0, qbase + 1)), cspec, cspec],
        out_specs=pl.BlockSpec((1, t, MEM_W), lambda bi: (bi, 0, 0)),
        out_shape=jax.ShapeDtypeStruct((b, t, MEM_W), BF16),
        compiler_params=_params("parallel"),
        name="mem_attention_cache",
    )(h3, h3, cache_k, cache_v)


def _merge_kernel(ca_ref, o0_ref, o1_ref, o2_ref, l0_ref, l1_ref, l2_ref, mo_ref, g0_ref, g1_ref, g2_ref,
                  wc_ref, ws_ref, wm_ref, out_ref):
    l0, l1, l2 = l0_ref[...], l1_ref[...], l2_ref[...]
    m = jnp.maximum(jnp.maximum(l0, l1), l2)
    e0, e1, e2 = jnp.exp(l0 - m), jnp.exp(l1 - m), jnp.exp(l2 - m)
    sw = (e0 * o0_ref[...] + e1 * o1_ref[...] + e2 * o2_ref[...]) / (e0 + e1 + e2)
    conv_out = _dot(ca_ref[...], wc_ref[...])
    sw_out = _dot(sw.astype(BF16), ws_ref[...])
    mem_out = _dot(mo_ref[...], wm_ref[...])
    merged = (jax.nn.sigmoid(g0_ref[...]) * conv_out + jax.nn.sigmoid(g1_ref[...]) * sw_out
              + jax.nn.sigmoid(g2_ref[...]) * mem_out)
    out_ref[...] = merged.astype(BF16)


def merge_branches(hcat, c_act, sw_o, sw_lse, mem_o, w_conv_o, w_sw_o, w_mem_o):
    n = hcat.shape[0]
    tm = _pick(n, (512, 256))
    tn = 512
    gbase = COL_GATE // tn
    gstep = D_MODEL // tn

    def row(width):
        return pl.BlockSpec((tm, width), lambda i, j: (i, 0))

    def gate(br):
        return pl.BlockSpec((tm, tn), lambda i, j: (i, gbase + br * gstep + j))

    def wcol(k):
        return pl.BlockSpec((k, tn), lambda i, j: (0, j))

    return pl.pallas_call(
        _merge_kernel,
        grid=(n // tm, D_MODEL // tn),
        in_specs=[row(C_CONV), row(SW_OUT), row(SW_OUT), row(SW_OUT), row(SW_OUT), row(SW_OUT), row(SW_OUT),
                  row(MEM_W), gate(0), gate(1), gate(2), wcol(C_CONV), wcol(SW_OUT), wcol(MEM_W)],
        out_specs=pl.BlockSpec((tm, tn), lambda i, j: (i, j)),
        out_shape=jax.ShapeDtypeStruct((n, D_MODEL), BF16),
        compiler_params=_params("parallel", "arbitrary"),
        name="merge_branches",
    )(c_act, *sw_o, *sw_lse, mem_o, hcat, hcat, hcat, w_conv_o, w_sw_o, w_mem_o)


def _top_values(s, rows16, k):
    vals = jnp.full(rows16.shape, NEG_INF, F32)
    work = s
    for i in range(k):
        mx = jnp.max(work, axis=0, keepdims=True)
        vals = jnp.where(rows16 == i, mx, vals)
        work = jnp.where(work == mx, NEG_INF, work)
    return vals


def _router_kernel(q_ref, k1_ref, k2_ref, s1_ref, s2_ref, e1_ref, e2_ref, tau_ref, *, tm):
    half = PEER_DKEY // 2
    rows16 = lax.broadcasted_iota(jnp.int32, (PEER_TOPK, tm), 0)
    rows8 = lax.broadcasted_iota(jnp.int32, (SUBLANES, tm), 0)
    k1 = k1_ref[...]
    k2 = k2_ref[...]

    def head(h, carry):
        c0 = pl.multiple_of(h * PEER_DKEY, PEER_DKEY)
        qa = q_ref[:, pl.ds(c0, half)].astype(BF16)
        qb = q_ref[:, pl.ds(c0 + half, half)].astype(BF16)
        s1 = _dot_nt(k1, qa)
        s2 = _dot_nt(k2, qb)
        s1 = s1 - jnp.max(s1, axis=0, keepdims=True)
        s2 = s2 - jnp.max(s2, axis=0, keepdims=True)
        v1 = _top_values(s1, rows16, PEER_TOPK)
        v2 = _top_values(s2, rows16, PEER_TOPK)
        parts = [v1[0:1] + v2]
        v2_top = v2[0:SUBLANES]
        for i in range(1, PEER_TOPK):
            keep = PEER_TOPK // (i + 1)
            parts.append(jnp.where(rows8 < keep, v1[i:i + 1] + v2_top, NEG_INF))
        cand = jnp.concatenate(parts, axis=0)
        work = cand
        tau = None
        for _i in range(PEER_TOPK):
            tau = jnp.max(work, axis=0, keepdims=True)
            work = jnp.where(work == tau, NEG_INF, work)
        z = jnp.sum(jnp.where(cand >= tau, jnp.exp(cand), 0.0), axis=0, keepdims=True)
        s1_ref[h] = s1
        s2_ref[h] = s2
        e1_ref[h] = jnp.exp(s1) * (1.0 / z)
        e2_ref[h] = jnp.exp(s2)
        tau_ref[pl.ds(h, 1), :] = tau
        return carry

    lax.fori_loop(0, PEER_HEADS, head, 0)


def peer_router(q, k1_bf, k2_bf):
    n = q.shape[0]
    tm = 256
    big = pl.BlockSpec((PEER_HEADS, PEER_NKEYS, tm), lambda i: (0, 0, i))
    big_sds = jax.ShapeDtypeStruct((PEER_HEADS, PEER_NKEYS, n), F32)
    return pl.pallas_call(
        functools.partial(_router_kernel, tm=tm),
        grid=(n // tm,),
        in_specs=[pl.BlockSpec((tm, PEER_HEADS * PEER_DKEY), lambda i: (i, 0)),
                  pl.BlockSpec((PEER_NKEYS, PEER_DKEY // 2), lambda i: (0, 0)),
                  pl.BlockSpec((PEER_NKEYS, PEER_DKEY // 2), lambda i: (0, 0))],
        out_specs=[big, big, big, big, pl.BlockSpec((PEER_HEADS, tm), lambda i: (0, i))],
        out_shape=[big_sds, big_sds, big_sds, big_sds, jax.ShapeDtypeStruct((PEER_HEADS, n), F32)],
        compiler_params=_params("parallel"),
        name="peer_router",
    )(q, k1_bf, k2_bf)


def _gelu_tanh(x):
    return 0.5 * x * (1.0 + jnp.tanh(math.sqrt(2.0 / math.pi) * (x + 0.044715 * (x * x * x))))


def _peer_kernel(h_ref, g2_ref, s1_ref, e1_ref, s2_ref, e2_ref, tau_ref, u_ref, vt_ref, y_ref,
                 hn_ref, acc_ref, act_new_ref, act_old_ref, coef_new_ref, coef_old_ref, *, tm, te, nj):
    s = pl.program_id(1)
    rows = te // PEER_NKEYS
    pack = 2 * SUBLANES
    chunks = [slice(t, t + PEER_TCHUNK) for t in range(0, tm, PEER_TCHUNK)]

    @pl.when(s == 0)
    def _():
        for ts in chunks:
            hn_ref[:, ts] = _rms(h_ref[ts, :], g2_ref[...]).T.astype(BF16)
        acc_ref[...] = jnp.zeros_like(acc_ref)
        act_old_ref[...] = jnp.zeros_like(act_old_ref)
        coef_old_ref[...] = jnp.zeros_like(coef_old_ref)

    upper = (jnp.clip(s - 1, 0, nj - 1) % 2) == 1

    def row_of(ref, h, al, cs):
        picked = jnp.where(upper, ref[h, rows + al:rows + al + 1, cs], ref[h, al:al + 1, cs])
        return jnp.broadcast_to(picked, (pack, LANES))

    def score(ts):
        act_new_ref[:, ts] = _dot(u_ref[...], hn_ref[:, ts])

    def accumulate(ts):
        acc_ref[:, ts] += _dot(vt_ref[...], coef_old_ref[:, ts])

    def gate_tile(al, c):
        cs = slice(c * LANES, (c + 1) * LANES)
        s1c, e1c, tau = [], [], []
        for h in range(PEER_HEADS):
            s1c.append(row_of(s1_ref, h, al, cs))
            e1c.append(row_of(e1_ref, h, al, cs))
            tau.append(jnp.broadcast_to(tau_ref[h:h + 1, cs], (pack, LANES)))
        for sb in range(PEER_NKEYS // pack):
            bs = slice(sb * pack, (sb + 1) * pack)
            gate = jnp.zeros((pack, LANES), F32)
            for h in range(PEER_HEADS):
                total = s1c[h] + s2_ref[h, bs, cs]
                gate = gate + jnp.where(total >= tau[h], e2_ref[h, bs, cs] * e1c[h], 0.0)
            er = slice(al * PEER_NKEYS + sb * pack, al * PEER_NKEYS + (sb + 1) * pack)
            coef_new_ref[er, cs] = (gate * _gelu_tanh(act_old_ref[er, cs])).astype(BF16)

    matmuls = []
    for ts in chunks:
        matmuls += [functools.partial(score, ts), functools.partial(accumulate, ts)]
    tiles = [(al, c) for al in range(rows) for c in range(tm // LANES)]
    per = -(-len(tiles) // len(matmuls))
    for k, mm in enumerate(matmuls):
        mm()
        for al, c in tiles[k * per:(k + 1) * per]:
            gate_tile(al, c)
    act_old_ref[...] = act_new_ref[...]
    coef_old_ref[...] = coef_new_ref[...]

    @pl.when(s == pl.num_programs(1) - 1)
    def _():
        for ts in chunks:
            y_ref[ts, :] = h_ref[ts, :] + acc_ref[:, ts].T


def peer_experts(h, norm2_g, routed, u_bf, vt_bf):
    n, d = h.shape
    n_exp = u_bf.shape[0]
    tm = _pick(n, (1024, 256))
    te = 512
    nj = n_exp // te
    rows = te // PEER_NKEYS
    assert 2 * rows == SUBLANES
    assert tm % PEER_TCHUNK == 0
    s1, s2, e1, e2, tau = routed

    def blk(off):
        return lambda i, s: jnp.clip(s - off, 0, nj - 1)

    once = pl.Buffered(1)
    row = pl.BlockSpec((PEER_HEADS, SUBLANES, tm), lambda i, s: (0, blk(1)(i, s) * rows // SUBLANES, i))
    big = pl.BlockSpec((PEER_HEADS, PEER_NKEYS, tm), lambda i, s: (0, 0, i), pipeline_mode=once)
    return pl.pallas_call(
        functools.partial(_peer_kernel, tm=tm, te=te, nj=nj),
        grid=(n // tm, nj + 2),
        in_specs=[pl.BlockSpec((tm, d), lambda i, s: (i, 0), pipeline_mode=once),
                  pl.BlockSpec((1, d), lambda i, s: (0, 0)),
                  row, row, big, big,
                  pl.BlockSpec((PEER_HEADS, tm), lambda i, s: (0, i)),
                  pl.BlockSpec((te, d), lambda i, s: (blk(0)(i, s), 0)),
                  pl.BlockSpec((d, te), lambda i, s: (0, blk(2)(i, s)))],
        out_specs=pl.BlockSpec((tm, d), lambda i, s: (i, 0), pipeline_mode=once),
        out_shape=jax.ShapeDtypeStruct((n, d), F32),
        scratch_shapes=[pltpu.VMEM((d, tm), BF16),
                        pltpu.VMEM((d, tm), F32),
                        pltpu.VMEM((te, tm), F32), pltpu.VMEM((te, tm), F32),
                        pltpu.VMEM((te, tm), BF16), pltpu.VMEM((te, tm), BF16)],
        compiler_params=_params("parallel", "arbitrary"),
        name="peer_experts",
    )(h, norm2_g.reshape(1, d), s1, e1, s2, e2, tau, u_bf, vt_bf)


def _in_proj_epilogue(sw_q_g, sw_k_g, mem_q_g):
    modes = np.zeros((IN_COLS // PROJ_TN,), np.int32)
    gain = [jnp.ones((IN_CONV,), F32)]
    for g in range(SW_GROUPS):
        t0 = (COL_SW + 3 * g * SW_OUT) // PROJ_TN
        modes[t0:t0 + 2 * SW_OUT // PROJ_TN] = TILE_NORM128
        gain += [jnp.tile(sw_q_g[g], SW_HG) * (SW_HD ** -0.5), jnp.tile(sw_k_g[g], SW_HG), jnp.ones((SW_OUT,), F32)]
    modes[COL_MEMQ // PROJ_TN:COL_GATE // PROJ_TN] = TILE_NORM256
    gain += [jnp.tile(mem_q_g, MEM_H) * (MEM_HD ** -0.5), jnp.ones((3 * D_MODEL,), F32)]
    return modes, jnp.concatenate(gain)


def _layer(x, conv_prev, sw_bufs, mem_k, mem_v, w):
    b, t, d = x.shape
    n = b * t
    x2 = x.reshape(n, d)
    hcat = norm_matmul(x2, w["norm1_g"], w["w_in"], *w["in_epilogue"])
    h3 = hcat.reshape(b, t, IN_COLS)
    c_act, conv_state = conv_branch(h3, conv_prev, w["conv_w"], w["conv_b"], w["conv_ln_g"], w["conv_ln_b"])
    sw_o, sw_lse, states = [], [], []
    for g in range(SW_GROUPS):
        if sw_bufs is None:
            o, lse = sw_prompt_group(h3, g)
            keep = min(SW_WINDOWS[g], t)
            for off in (1, 2):
                col = COL_SW + (3 * g + off) * SW_OUT
                states.append(h3[:, t - keep:, col:col + SW_OUT].reshape(b, keep, SW_HG, SW_HD))
        else:
            o, lse, k_new, v_new = sw_sample_group(h3, g, sw_bufs[2 * g], sw_bufs[2 * g + 1])
            states += [k_new, v_new]
        sw_o.append(o.reshape(n, SW_OUT))
        sw_lse.append(lse.reshape(n, SW_OUT))
    if sw_bufs is None:
        mem_o = mem_attention(h3, mem_k, mem_v)
    else:
        mem_o = mem_attention_cache(h3, mem_k, mem_v)
    merged = merge_branches(hcat, c_act.reshape(n, C_CONV), sw_o, sw_lse, mem_o.reshape(n, MEM_W),
                            w["w_conv_o"], w["w_sw_o"], w["w_mem_o"])
    hres = matmul_residual(merged, w["w_out"], x2)
    q = norm_matmul(hres, w["norm2_g"], w["peer_w_q"])
    routed = peer_router(q, w["peer_k1"], w["peer_k2"])
    y = peer_experts(hres, w["norm2_g"], routed, w["peer_u"], w["peer_vt"])
    return y.reshape(b, t, d), conv_state, states


def kernel(x_prompt, x_sample, mem_prompt, state_conv, state_sw0_k, state_sw0_v, state_sw1_k, state_sw1_v, state_sw2_k, state_sw2_v, cache_mem_k, cache_mem_v, norm1_g, mem_norm_g, w_in, conv_w, conv_b, conv_ln_g, conv_ln_b, w_conv_o, sw_q_g, sw_k_g, w_sw_o, w_mem_k, w_mem_v, mem_q_g, mem_k_g, w_mem_o, w_out, norm2_g, peer_w_q, peer_k1, peer_k2, peer_u, peer_v):
    depth = w_in.shape[0]
    bp, tp, _ = x_prompt.shape
    n_mem = mem_prompt.shape[1]
    sw_states = (state_sw0_k, state_sw0_v, state_sw1_k, state_sw1_v, state_sw2_k, state_sw2_v)
    yp, ys = x_prompt, x_sample
    conv_p, conv_s, memk_p, memv_p = [], [], [], []
    sw_p = [[] for _ in range(2 * SW_GROUPS)]
    sw_s = [[] for _ in range(2 * SW_GROUPS)]
    for l in range(depth):
        w = dict(
            norm1_g=norm1_g[l], w_in=w_in[l].astype(BF16), conv_w=conv_w[l], conv_b=conv_b[l],
            conv_ln_g=conv_ln_g[l], conv_ln_b=conv_ln_b[l], w_conv_o=w_conv_o[l].astype(BF16),
            w_sw_o=w_sw_o[l].astype(BF16), w_mem_o=w_mem_o[l].astype(BF16), w_out=w_out[l].astype(BF16),
            norm2_g=norm2_g[l], peer_w_q=peer_w_q[l].astype(BF16), peer_k1=peer_k1[l].astype(BF16),
            peer_k2=peer_k2[l].astype(BF16), peer_u=peer_u[l].astype(BF16), peer_vt=peer_v[l].T.astype(BF16),
            in_epilogue=_in_proj_epilogue(sw_q_g[l], sw_k_g[l], mem_q_g[l]))
        w_kv = jnp.concatenate([w_mem_k[l], w_mem_v[l]], axis=1).astype(BF16)
        kv_modes = np.array([TILE_NORM256] * (MEM_W // PROJ_TN) + [TILE_PLAIN] * (MEM_W // PROJ_TN), np.int32)
        kv_gain = jnp.concatenate([jnp.tile(mem_k_g[l], MEM_H), jnp.ones((MEM_W,), F32)])
        kv = norm_matmul(mem_prompt.reshape(bp * n_mem, D_MODEL), mem_norm_g[l], w_kv, kv_modes, kv_gain)
        mk = kv[:, :MEM_W].reshape(bp, n_mem, MEM_W)
        mv = kv[:, MEM_W:].reshape(bp, n_mem, MEM_W)

        zeros_prev = jnp.zeros((bp, CONV_W - 1, C_CONV), F32)
        yp, cp, swp_l = _layer(yp, zeros_prev, None, mk, mv, w)
        ys, cs, sws_l = _layer(ys, state_conv[l], [s[l] for s in sw_states], cache_mem_k[l], cache_mem_v[l], w)
        conv_p.append(cp)
        conv_s.append(cs)
        memk_p.append(mk.reshape(bp, n_mem, MEM_H, MEM_HD))
        memv_p.append(mv.reshape(bp, n_mem, MEM_H, MEM_HD))
        for i in range(2 * SW_GROUPS):
            sw_p[i].append(swp_l[i])
            sw_s[i].append(sws_l[i])
    swp = [jnp.stack(a, axis=0) for a in sw_p]
    sws = [jnp.stack(a, axis=0) for a in sw_s]
    return (yp, ys, jnp.stack(conv_p, axis=0), jnp.stack(conv_s, axis=0),
            swp[0], swp[1], swp[2], swp[3], swp[4], swp[5],
            sws[0], sws[1], sws[2], sws[3], sws[4], sws[5],
            jnp.stack(memk_p, axis=0), jnp.stack(memv_p, axis=0))
```

```python
import functools
import math

import numpy as np
import jax
import jax.numpy as jnp
from jax import lax
from jax.experimental import pallas as pl
from jax.experimental.pallas import tpu as pltpu

F32 = jnp.float32
BF16 = jnp.bfloat16
EPS = 1e-6
NEG_INF = float("-inf")

LANES = 128
SUBLANES = 8
VMEM_LIMIT = 60 * 1024 * 1024

D_MODEL = 2048
C_CONV = 1024
CONV_W = 31
CONV_PAD = 32
SW_GROUPS = 3
SW_DILATIONS = (1, 4, 16)
SW_WINDOWS = (128, 512, 2048)
SW_SPAN = 128
SW_HG = 4
SW_HD = 128
SW_OUT = SW_HG * SW_HD
MEM_H = 4
MEM_HD = 256
MEM_W = MEM_H * MEM_HD
IN_CONV = 2 * C_CONV
IN_SW = SW_GROUPS * 3 * SW_OUT
IN_COLS = IN_CONV + IN_SW + MEM_W + 3 * D_MODEL
COL_SW = IN_CONV
COL_MEMQ = IN_CONV + IN_SW
COL_GATE = COL_MEMQ + MEM_W
PROJ_TN = 512
PEER_HEADS = 8
PEER_NKEYS = 128
PEER_DKEY = 256
PEER_TOPK = 16
PEER_TCHUNK = 256

TILE_PLAIN, TILE_NORM128, TILE_NORM256 = 0, 1, 2
_TILE_HD = {TILE_NORM128: SW_HD, TILE_NORM256: MEM_HD}


def _alibi_slopes():
    n = SW_GROUPS * SW_HG
    s = np.float32(2.0) ** (np.float32(-8.0) * np.arange(1, n + 1, dtype=np.float32) / np.float32(n))
    return s.reshape(SW_GROUPS, SW_HG)


def _rms(x, g):
    return x * lax.rsqrt(jnp.mean(x * x, axis=-1, keepdims=True) + EPS) * g


def _dot(a, b):
    return jnp.dot(a, b, preferred_element_type=F32)


def _dot_nt(a, b):
    return lax.dot_general(a, b, (((1,), (1,)), ((), ())), preferred_element_type=F32)


def _params(*sem):
    return pltpu.CompilerParams(dimension_semantics=sem, vmem_limit_bytes=VMEM_LIMIT)


def _pick(n, options):
    for o in options:
        if n % o == 0:
            return o
    raise ValueError(f"no tile in {options} divides {n}")


def _norm_matmul_kernel(mode_ref, x_ref, g_ref, w_ref, cg_ref, o_ref, xn_ref, *, tn, head_modes):
    j = pl.program_id(1)

    @pl.when(j == 0)
    def _():
        xn_ref[...] = _rms(x_ref[...], g_ref[...]).astype(BF16)

    acc = _dot(xn_ref[...], w_ref[...])
    if not head_modes:
        o_ref[...] = acc
        return
    mode = mode_ref[j]

    @pl.when(mode == TILE_PLAIN)
    def _():
        o_ref[...] = acc

    for code in head_modes:
        hd = _TILE_HD[code]

        @pl.when(mode == code)
        def _():
            for h in range(tn // hd):
                sl = slice(h * hd, (h + 1) * hd)
                o_ref[:, sl] = _rms(acc[:, sl], cg_ref[:, sl])


def norm_matmul(x, g, w_bf, tile_modes=None, col_gain=None):
    n, d = x.shape
    cols = w_bf.shape[1]
    tm = _pick(n, (1024, 256))
    tn = PROJ_TN
    nt = cols // tn
    if tile_modes is None:
        tile_modes = np.zeros((nt,), np.int32)
        col_gain = jnp.ones((cols,), F32)
    head_modes = tuple(sorted(set(int(m) for m in tile_modes) - {TILE_PLAIN}))
    grid_spec = pltpu.PrefetchScalarGridSpec(
        num_scalar_prefetch=1,
        grid=(n // tm, nt),
        in_specs=[pl.BlockSpec((tm, d), lambda i, j, m: (i, 0)),
                  pl.BlockSpec((1, d), lambda i, j, m: (0, 0)),
                  pl.BlockSpec((d, tn), lambda i, j, m: (0, j)),
                  pl.BlockSpec((1, tn), lambda i, j, m: (0, j))],
        out_specs=pl.BlockSpec((tm, tn), lambda i, j, m: (i, j)),
        scratch_shapes=[pltpu.VMEM((tm, d), BF16)])
    return pl.pallas_call(
        functools.partial(_norm_matmul_kernel, tn=tn, head_modes=head_modes),
        grid_spec=grid_spec,
        out_shape=jax.ShapeDtypeStruct((n, cols), F32),
        compiler_params=_params("parallel", "arbitrary"),
        name="norm_matmul",
    )(jnp.asarray(tile_modes, jnp.int32), x, g.reshape(1, d), w_bf, col_gain.reshape(1, cols))


def _matmul_res_kernel(a_ref, w_ref, x_ref, o_ref):
    o_ref[...] = x_ref[...] + _dot(a_ref[...], w_ref[...])


def matmul_residual(a_bf, w_bf, x):
    n, k = a_bf.shape
    cols = w_bf.shape[1]
    tm = _pick(n, (512, 256))
    tn = _pick(cols, (1024,))
    return pl.pallas_call(
        _matmul_res_kernel,
        grid=(n // tm, cols // tn),
        in_specs=[pl.BlockSpec((tm, k), lambda i, j: (i, 0)),
                  pl.BlockSpec((k, tn), lambda i, j: (0, j)),
                  pl.BlockSpec((tm, tn), lambda i, j: (i, j))],
        out_specs=pl.BlockSpec((tm, tn), lambda i, j: (i, j)),
        out_shape=jax.ShapeDtypeStruct((n, cols), F32),
        compiler_params=_params("parallel", "arbitrary"),
        name="matmul_residual",
    )(a_bf, w_bf, x)


def _conv_kernel(hc_ref, prev_ref, cw_ref, cb_ref, lg_ref, lb_ref, act_ref, st_ref, ext_ref, c_ref, *, tt):
    i = pl.program_id(1)
    hist = CONV_W - 1
    lo = CONV_PAD - hist

    @pl.when(i == 0)
    def _():
        ext_ref[lo:CONV_PAD, :] = prev_ref[0]

    @pl.when(i > 0)
    def _():
        ext_ref[lo:CONV_PAD, :] = ext_ref[tt + lo:tt + CONV_PAD, :]

    a = hc_ref[0, :, :C_CONV]
    b = hc_ref[0, :, C_CONV:]
    ext_ref[CONV_PAD:CONV_PAD + tt, :] = a * jax.nn.sigmoid(b)

    for c in range(C_CONV // LANES):
        cs = slice(c * LANES, (c + 1) * LANES)
        acc = jnp.zeros((tt, LANES), F32)
        for w in range(CONV_W):
            acc = acc + ext_ref[lo + w:lo + w + tt, cs] * cw_ref[w:w + 1, cs]
        c_ref[:, cs] = acc + cb_ref[:, cs]

    cv = c_ref[...]
    mu = jnp.mean(cv, axis=-1, keepdims=True)
    xc = cv - mu
    var = jnp.mean(xc * xc, axis=-1, keepdims=True)
    y = xc * lax.rsqrt(var + EPS) * lg_ref[...] + lb_ref[...]
    act_ref[0] = (y * jax.nn.sigmoid(y)).astype(BF16)

    @pl.when(i == pl.num_programs(1) - 1)
    def _():
        st_ref[0] = ext_ref[tt + lo:tt + CONV_PAD, :]


def conv_branch(h3, prev, conv_w, conv_b, ln_g, ln_b):
    b, t, _ = h3.shape
    tt = _pick(t, (128, 8))
    hist = CONV_W - 1
    return pl.pallas_call(
        functools.partial(_conv_kernel, tt=tt),
        grid=(b, t // tt),
        in_specs=[pl.BlockSpec((1, tt, IN_CONV), lambda bi, i: (bi, i, 0)),
                  pl.BlockSpec((1, hist, C_CONV), lambda bi, i: (bi, 0, 0)),
                  pl.BlockSpec((CONV_W, C_CONV), lambda bi, i: (0, 0)),
                  pl.BlockSpec((1, C_CONV), lambda bi, i: (0, 0)),
                  pl.BlockSpec((1, C_CONV), lambda bi, i: (0, 0)),
                  pl.BlockSpec((1, C_CONV), lambda bi, i: (0, 0))],
        out_specs=[pl.BlockSpec((1, tt, C_CONV), lambda bi, i: (bi, i, 0)),
                   pl.BlockSpec((1, hist, C_CONV), lambda bi, i: (bi, 0, 0))],
        out_shape=[jax.ShapeDtypeStruct((b, t, C_CONV), BF16),
                   jax.ShapeDtypeStruct((b, hist, C_CONV), F32)],
        scratch_shapes=[pltpu.VMEM((CONV_PAD + tt, C_CONV), F32),
                        pltpu.VMEM((tt, C_CONV), F32)],
        compiler_params=_params("parallel", "arbitrary"),
        name="conv_branch",
    )(h3, prev, conv_w, conv_b.reshape(1, C_CONV), ln_g.reshape(1, C_CONV), ln_b.reshape(1, C_CONV))


def _with_ones(v):
    return jnp.concatenate([v.astype(BF16), jnp.ones(v.shape, BF16)], axis=-1)


def _softmax_pv(s_a, s_b, v_a, v_b):
    if s_a.shape == s_b.shape:
        m = jnp.max(jnp.maximum(s_a, s_b), axis=-1, keepdims=True)
    else:
        m = jnp.maximum(jnp.max(s_a, axis=-1, keepdims=True), jnp.max(s_b, axis=-1, keepdims=True))
    p_a = jnp.exp(s_a - m).astype(BF16)
    p_b = jnp.exp(s_b - m).astype(BF16)
    hd = v_a.shape[-1]
    ol = _dot(p_a, _with_ones(v_a)) + _dot(p_b, _with_ones(v_b))
    l = ol[:, hd:]
    return ol[:, :hd] / l, m + jnp.log(l)


def _sw_prompt_kernel(slope_ref, q_ref, kc_ref, kp_ref, vc_ref, vp_ref, o_ref, lse_ref, *, r, nq):
    h = pl.program_id(1)
    n = pl.program_id(2)
    blk = SW_SPAN
    slope = slope_ref[h]
    ii = lax.broadcasted_iota(jnp.int32, (blk, blk), 0)
    cc = lax.broadcasted_iota(jnp.int32, (blk, blk), 1)
    back_c = ii - cc
    back_p = back_c + blk
    bias_c = jnp.where(back_c >= 0, -slope * (r * back_c).astype(F32), NEG_INF)
    bias_p = jnp.where(back_p <= SW_SPAN, -slope * (r * back_p).astype(F32), NEG_INF)
    bias_first = bias_p + jnp.where(n >= 1, 0.0, NEG_INF).astype(F32)

    def rows(rho, m):
        start = rho + r * blk * m
        return pl.ds(start, blk) if r == 1 else pl.ds(start, blk, stride=r)

    for rho in range(r):
        for m in range(nq):
            q = q_ref[0, rows(rho, m), :].astype(BF16)
            k_c = kc_ref[0, rows(rho, m), :]
            v_c = vc_ref[0, rows(rho, m), :]
            if m == 0:
                k_p = kp_ref[0, rows(rho, 0), :]
                v_p = vp_ref[0, rows(rho, 0), :]
                b_p = bias_first
            else:
                k_p = kc_ref[0, rows(rho, m - 1), :]
                v_p = vc_ref[0, rows(rho, m - 1), :]
                b_p = bias_p
            s_c = _dot_nt(q, k_c.astype(BF16)) + bias_c
            s_p = _dot_nt(q, k_p.astype(BF16)) + b_p
            o, lse = _softmax_pv(s_c, s_p, v_c, v_p)
            o_ref[0, rows(rho, m), :] = o
            lse_ref[0, rows(rho, m), :] = lse


def sw_prompt_group(h3, g):
    b, t, _ = h3.shape
    r = SW_DILATIONS[g]
    nq = max(1, 4 // r)
    pb = SW_SPAN * r
    ch = pb * nq
    assert t % ch == 0
    base = (COL_SW + 3 * g * SW_OUT) // SW_HD
    hpb = SW_OUT // SW_HD
    slopes = jnp.asarray(_alibi_slopes()[g])

    def cur(off):
        return pl.BlockSpec((1, ch, SW_HD), lambda bi, h, n: (bi, n, base + off * hpb + h))

    def prev(off):
        return pl.BlockSpec((1, pb, SW_HD), lambda bi, h, n: (bi, jnp.maximum(n * nq - 1, 0), base + off * hpb + h))

    out_spec = pl.BlockSpec((1, ch, SW_HD), lambda bi, h, n: (bi, n, h))
    out_sds = jax.ShapeDtypeStruct((b, t, SW_OUT), F32)
    return pl.pallas_call(
        functools.partial(_sw_prompt_kernel, r=r, nq=nq),
        grid=(b, SW_HG, t // ch),
        in_specs=[pl.BlockSpec(memory_space=pltpu.SMEM),
                  cur(0), cur(1), prev(1), cur(2), prev(2)],
        out_specs=[out_spec, out_spec],
        out_shape=[out_sds, out_sds],
        compiler_params=_params("parallel", "parallel", "arbitrary"),
        name=f"sw_prompt_g{g}",
    )(slopes, h3, h3, h3, h3, h3)


def _sw_sample_kernel(q_ref, kn_ref, vn_ref, kb_ref, vb_ref, o_ref, lse_ref, ko_ref, vo_ref,
                      *, r, buf_len, t_new, slopes):
    nh = SW_HG
    span = SW_SPAN
    shift = r.bit_length() - 1
    first = buf_len - span * r
    tq = lax.broadcasted_iota(jnp.int32, (t_new, span), 0)
    mm = lax.broadcasted_iota(jnp.int32, (t_new, span), 1)
    back_b = span + (tq >> shift) - mm
    ok_b = back_b <= span
    dist_bf = (r * back_b).astype(F32)
    cls = tq & (r - 1)
    tq2 = lax.broadcasted_iota(jnp.int32, (t_new, t_new), 0)
    tk2 = lax.broadcasted_iota(jnp.int32, (t_new, t_new), 1)
    dist_n = tq2 - tk2
    ok_n = (dist_n >= 0) & ((dist_n & (r - 1)) == 0)
    dist_nf = dist_n.astype(F32)
    keep = (buf_len - t_new) * nh
    ko_ref[pl.ds(0, keep), :] = kb_ref[pl.ds(t_new * nh, keep), :]
    vo_ref[pl.ds(0, keep), :] = vb_ref[pl.ds(t_new * nh, keep), :]
    classes = range(min(r, t_new))
    for h in range(nh):
        sl = slice(h * SW_HD, (h + 1) * SW_HD)
        q = q_ref[0, :, sl].astype(BF16)
        k_n = kn_ref[0, :, sl]
        v_n = vn_ref[0, :, sl]
        ko_ref[pl.ds(keep + h, t_new, stride=nh), :] = k_n
        vo_ref[pl.ds(keep + h, t_new, stride=nh), :] = v_n

        def class_rows(rho):
            return pl.ds((first + rho) * nh + h, span, stride=r * nh)

        s_b = None
        for rho in classes:
            s_c = _dot_nt(q, kb_ref[class_rows(rho), :].astype(BF16))
            s_b = s_c if s_b is None else jnp.where(cls == rho, s_c, s_b)
        s_b = jnp.where(ok_b, s_b - slopes[h] * dist_bf, NEG_INF)
        s_n = jnp.where(ok_n, _dot_nt(q, k_n.astype(BF16)) - slopes[h] * dist_nf, NEG_INF)
        m = jnp.maximum(jnp.max(s_b, axis=-1, keepdims=True), jnp.max(s_n, axis=-1, keepdims=True))
        p_b = jnp.exp(s_b - m)
        ol = _dot(jnp.exp(s_n - m).astype(BF16), _with_ones(v_n))
        for rho in classes:
            p_c = p_b if len(classes) == 1 else jnp.where(cls == rho, p_b, 0.0)
            ol = ol + _dot(p_c.astype(BF16), _with_ones(vb_ref[class_rows(rho), :]))
        l = ol[:, SW_HD:]
        o_ref[0, :, sl] = ol[:, :SW_HD] / l
        lse_ref[0, :, sl] = m + jnp.log(l)


def sw_sample_group(h3, g, k_buf, v_buf):
    b, t, _ = h3.shape
    buf_len = k_buf.shape[1]
    r = SW_DILATIONS[g]
    assert buf_len >= SW_SPAN * r and buf_len % r == 0 and t <= r * SW_SPAN and buf_len > t
    base = COL_SW // SW_OUT + 3 * g
    blk = (1, t, SW_OUT)
    rows = buf_len * SW_HG
    slopes = tuple(float(s) for s in _alibi_slopes()[g])
    out_spec = pl.BlockSpec(blk, lambda bi: (bi, 0, 0))
    buf_spec = pl.BlockSpec((None, rows, SW_HD), lambda bi: (bi, 0, 0))
    out_sds = jax.ShapeDtypeStruct((b, t, SW_OUT), F32)
    buf_sds = jax.ShapeDtypeStruct((b, rows, SW_HD), F32)
    k_buf = k_buf.reshape(b, rows, SW_HD)
    v_buf = v_buf.reshape(b, rows, SW_HD)
    o, lse, k_out, v_out = pl.pallas_call(
        functools.partial(_sw_sample_kernel, r=r, buf_len=buf_len, t_new=t, slopes=slopes),
        grid=(b,),
        in_specs=[pl.BlockSpec(blk, lambda bi: (bi, 0, base)),
                  pl.BlockSpec(blk, lambda bi: (bi, 0, base + 1)),
                  pl.BlockSpec(blk, lambda bi: (bi, 0, base + 2)),
                  buf_spec, buf_spec],
        out_specs=[out_spec, out_spec, buf_spec, buf_spec],
        out_shape=[out_sds, out_sds, buf_sds, buf_sds],
        compiler_params=_params("parallel"),
        name=f"sw_sample_g{g}",
    )(h3, h3, h3, k_buf, v_buf)
    shape4 = (b, buf_len, SW_HG, SW_HD)
    return o, lse, k_out.reshape(shape4), v_out.reshape(shape4)


def _attend(q, k, v):
    s = _dot_nt(q.astype(BF16), k.astype(BF16))
    m = jnp.max(s, axis=-1, keepdims=True)
    p = jnp.exp(s - m)
    p = p / jnp.sum(p, axis=-1, keepdims=True)
    return _dot(p.astype(BF16), v.astype(BF16)).astype(BF16)


def _mem_attn_kernel(q_ref, k_ref, v_ref, o_ref):
    o_ref[0] = _attend(q_ref[0], k_ref[0], v_ref[0])


def mem_attention(h3, mem_k, mem_v):
    b, t, _ = h3.shape
    n_mem = mem_k.shape[1]
    tq = _pick(t, (512,))
    qbase = COL_MEMQ // MEM_HD
    return pl.pallas_call(
        _mem_attn_kernel,
        grid=(b, t // tq, MEM_H),
        in_specs=[pl.BlockSpec((1, tq, MEM_HD), lambda bi, i, h: (bi, i, qbase + h)),
                  pl.BlockSpec((1, n_mem, MEM_HD), lambda bi, i, h: (bi, 0, h)),
                  pl.BlockSpec((1, n_mem, MEM_HD), lambda bi, i, h: (bi, 0, h))],
        out_specs=pl.BlockSpec((1, tq, MEM_HD), lambda bi, i, h: (bi, i, h)),
        out_shape=jax.ShapeDtypeStruct((b, t, MEM_W), BF16),
        compiler_params=_params("parallel", "parallel", "arbitrary"),
        name="mem_attention",
    )(h3, mem_k, mem_v)


def _mem_attn_cache_kernel(qa_ref, qb_ref, k_ref, v_ref, o_ref):
    per = PROJ_TN // MEM_HD
    for h in range(MEM_H):
        q_ref = (qa_ref, qb_ref)[h // per]
        ql = slice((h % per) * MEM_HD, (h % per + 1) * MEM_HD)
        sl = slice(h * MEM_HD, (h + 1) * MEM_HD)
        o_ref[0, :, sl] = _attend(q_ref[0, :, ql], k_ref[:, h, :], v_ref[:, h, :])


def mem_attention_cache(h3, cache_k, cache_v):
    b, t, _ = h3.shape
    n_mem = cache_k.shape[1]
    assert MEM_W == 2 * PROJ_TN
    qbase = COL_MEMQ // PROJ_TN
    cspec = pl.BlockSpec((None, n_mem, MEM_H, MEM_HD), lambda bi: (bi, 0, 0, 0))
    return pl.pallas_call(
        _mem_attn_cache_kernel,
        grid=(b,),
        in_specs=[pl.BlockSpec((1, t, PROJ_TN), lambda bi: (bi, 0, qbase)),
                  pl.BlockSpec((1, t, PROJ_TN), lambda bi: (bi, 0, qbase + 1)), cspec, cspec],
        out_specs=pl.BlockSpec((1, t, MEM_W), lambda bi: (bi, 0, 0)),
        out_shape=jax.ShapeDtypeStruct((b, t, MEM_W), BF16),
        compiler_params=_params("parallel"),
        name="mem_attention_cache",
    )(h3, h3, cache_k, cache_v)


def _merge_kernel(ca_ref, o0_ref, o1_ref, o2_ref, l0_ref, l1_ref, l2_ref, mo_ref, g0_ref, g1_ref, g2_ref,
                  wc_ref, ws_ref, wm_ref, out_ref):
    l0, l1, l2 = l0_ref[...], l1_ref[...], l2_ref[...]
    m = jnp.maximum(jnp.maximum(l0, l1), l2)
    e0, e1, e2 = jnp.exp(l0 - m), jnp.exp(l1 - m), jnp.exp(l2 - m)
    sw = (e0 * o0_ref[...] + e1 * o1_ref[...] + e2 * o2_ref[...]) / (e0 + e1 + e2)
    conv_out = _dot(ca_ref[...], wc_ref[...])
    sw_out = _dot(sw.astype(BF16), ws_ref[...])
    mem_out = _dot(mo_ref[...], wm_ref[...])
    merged = (jax.nn.sigmoid(g0_ref[...]) * conv_out + jax.nn.sigmoid(g1_ref[...]) * sw_out
              + jax.nn.sigmoid(g2_ref[...]) * mem_out)
    out_ref[...] = merged.astype(BF16)


def merge_branches(hcat, c_act, sw_o, sw_lse, mem_o, w_conv_o, w_sw_o, w_mem_o):
    n = hcat.shape[0]
    tm = _pick(n, (512, 256))
    tn = 512
    gbase = COL_GATE // tn
    gstep = D_MODEL // tn

    def row(width):
        return pl.BlockSpec((tm, width), lambda i, j: (i, 0))

    def gate(br):
        return pl.BlockSpec((tm, tn), lambda i, j: (i, gbase + br * gstep + j))

    def wcol(k):
        return pl.BlockSpec((k, tn), lambda i, j: (0, j))

    return pl.pallas_call(
        _merge_kernel,
        grid=(n // tm, D_MODEL // tn),
        in_specs=[row(C_CONV), row(SW_OUT), row(SW_OUT), row(SW_OUT), row(SW_OUT), row(SW_OUT), row(SW_OUT),
                  row(MEM_W), gate(0), gate(1), gate(2), wcol(C_CONV), wcol(SW_OUT), wcol(MEM_W)],
        out_specs=pl.BlockSpec((tm, tn), lambda i, j: (i, j)),
        out_shape=jax.ShapeDtypeStruct((n, D_MODEL), BF16),
        compiler_params=_params("parallel", "arbitrary"),
        name="merge_branches",
    )(c_act, *sw_o, *sw_lse, mem_o, hcat, hcat, hcat, w_conv_o, w_sw_o, w_mem_o)


def _sorting_network(n):
    def merge(lo, hi, r):
        step = 2 * r
        if step < hi - lo:
            yield from merge(lo, hi, step)
            yield from merge(lo + r, hi, step)
            yield from ((i, i + r) for i in range(lo + r, hi - r, step))
        else:
            yield (lo, lo + r)

    def sort(lo, hi):
        if hi > lo:
            mid = lo + (hi - lo) // 2
            yield from sort(lo, mid)
            yield from sort(mid + 1, hi)
            yield from merge(lo, hi, 1)

    return tuple(sort(0, n - 1))


def _pop_heads(lists, popped):
    return [jnp.where(popped, lists[d + 1], lists[d]) for d in range(len(lists) - 1)]


def _top_values(s, rows16, k):
    lists = [s[i * SUBLANES:(i + 1) * SUBLANES, :] for i in range(s.shape[0] // SUBLANES)]
    assert len(lists) == k
    for i, j in _sorting_network(len(lists)):
        lists[i], lists[j] = jnp.maximum(lists[i], lists[j]), jnp.minimum(lists[i], lists[j])
    vals = jnp.full(rows16.shape, NEG_INF, F32)
    for i in range(k):
        mx = jnp.max(lists[0], axis=0, keepdims=True)
        vals = jnp.where(rows16 == i, mx, vals)
        lists = _pop_heads(lists, lists[0] == mx)
    return vals


def _kth_pair_sum(v1, v2, rows8, k):
    lo = [jnp.where((rows8 + 1) * (j + 1) <= k, v1[:SUBLANES] + v2[j:j + 1], NEG_INF) for j in range(k)]
    hi = v1[SUBLANES:] + v2[0:1]
    cands = lo + [hi]
    tau = None
    for _ in range(k):
        tau = jnp.max(jnp.maximum(lo[0], hi), axis=0, keepdims=True)
        hi = jnp.where(hi == tau, NEG_INF, hi)
        lo = _pop_heads(lo, lo[0] == tau) if len(lo) > 1 else lo
    return tau, cands


def _router_kernel(q_ref, k1_ref, k2_ref, s1_ref, s2_ref, e1_ref, e2_ref, tau_ref, *, tm):
    half = PEER_DKEY // 2
    rows16 = lax.broadcasted_iota(jnp.int32, (PEER_TOPK, tm), 0)
    rows8 = lax.broadcasted_iota(jnp.int32, (SUBLANES, tm), 0)
    k1 = k1_ref[...]
    k2 = k2_ref[...]

    def head(h, carry):
        c0 = pl.multiple_of(h * PEER_DKEY, PEER_DKEY)
        qa = q_ref[:, pl.ds(c0, half)].astype(BF16)
        qb = q_ref[:, pl.ds(c0 + half, half)].astype(BF16)
        s1 = _dot_nt(k1, qa)
        s2 = _dot_nt(k2, qb)
        s1 = s1 - jnp.max(s1, axis=0, keepdims=True)
        s2 = s2 - jnp.max(s2, axis=0, keepdims=True)
        v1 = _top_values(s1, rows16, PEER_TOPK)
        v2 = _top_values(s2, rows16, PEER_TOPK)
        tau, cands = _kth_pair_sum(v1, v2, rows8, PEER_TOPK)
        z = jnp.zeros((SUBLANES, tm), F32)
        for c in cands:
            z = z + jnp.where(c >= tau, jnp.exp(c), 0.0)
        z = jnp.sum(z, axis=0, keepdims=True)
        s1_ref[h] = s1
        s2_ref[h] = s2
        e1_ref[h] = jnp.exp(s1) * (1.0 / z)
        e2_ref[h] = jnp.exp(s2)
        tau_ref[pl.ds(h, 1), :] = tau
        return carry

    lax.fori_loop(0, PEER_HEADS, head, 0)


def peer_router(q, k1_bf, k2_bf):
    n = q.shape[0]
    tm = 256
    big = pl.BlockSpec((PEER_HEADS, PEER_NKEYS, tm), lambda i: (0, 0, i))
    big_sds = jax.ShapeDtypeStruct((PEER_HEADS, PEER_NKEYS, n), F32)
    return pl.pallas_call(
        functools.partial(_router_kernel, tm=tm),
        grid=(n // tm,),
        in_specs=[pl.BlockSpec((tm, PEER_HEADS * PEER_DKEY), lambda i: (i, 0)),
                  pl.BlockSpec((PEER_NKEYS, PEER_DKEY // 2), lambda i: (0, 0)),
                  pl.BlockSpec((PEER_NKEYS, PEER_DKEY // 2), lambda i: (0, 0))],
        out_specs=[big, big, big, big, pl.BlockSpec((PEER_HEADS, tm), lambda i: (0, i))],
        out_shape=[big_sds, big_sds, big_sds, big_sds, jax.ShapeDtypeStruct((PEER_HEADS, n), F32)],
        compiler_params=_params("parallel"),
        name="peer_router",
    )(q, k1_bf, k2_bf)


def _gelu_tanh(x):
    return 0.5 * x * (1.0 + jnp.tanh(math.sqrt(2.0 / math.pi) * (x + 0.044715 * (x * x * x))))


def _peer_kernel(h_ref, g2_ref, s1_ref, e1_ref, s2_ref, e2_ref, tau_ref, u_ref, vt_ref, y_ref,
                 hn_ref, acc_ref, act_new_ref, act_old_ref, coef_new_ref, coef_old_ref, *, tm, te, nj):
    s = pl.program_id(1)
    rows = te // PEER_NKEYS
    pack = 2 * SUBLANES
    chunks = [slice(t, t + PEER_TCHUNK) for t in range(0, tm, PEER_TCHUNK)]

    @pl.when(s == 0)
    def _():
        for ts in chunks:
            hn_ref[:, ts] = _rms(h_ref[ts, :], g2_ref[...]).T.astype(BF16)
        acc_ref[...] = jnp.zeros_like(acc_ref)
        act_old_ref[...] = jnp.zeros_like(act_old_ref)
        coef_old_ref[...] = jnp.zeros_like(coef_old_ref)

    upper = (jnp.clip(s - 1, 0, nj - 1) % 2) == 1

    def row_of(ref, h, al, cs):
        picked = jnp.where(upper, ref[h, rows + al:rows + al + 1, cs], ref[h, al:al + 1, cs])
        return jnp.broadcast_to(picked, (pack, LANES))

    def score(ts):
        act_new_ref[:, ts] = _dot(u_ref[...], hn_ref[:, ts])

    def accumulate(ts):
        acc_ref[:, ts] += _dot(vt_ref[...], coef_old_ref[:, ts])

    def gate_tile(al, c):
        cs = slice(c * LANES, (c + 1) * LANES)
        s1c, e1c, tau = [], [], []
        for h in range(PEER_HEADS):
            s1c.append(row_of(s1_ref, h, al, cs))
            e1c.append(row_of(e1_ref, h, al, cs))
            tau.append(jnp.broadcast_to(tau_ref[h:h + 1, cs], (pack, LANES)))
        for sb in range(PEER_NKEYS // pack):
            bs = slice(sb * pack, (sb + 1) * pack)
            gate = jnp.zeros((pack, LANES), F32)
            for h in range(PEER_HEADS):
                total = s1c[h] + s2_ref[h, bs, cs]
                gate = gate + jnp.where(total >= tau[h], e2_ref[h, bs, cs] * e1c[h], 0.0)
            er = slice(al * PEER_NKEYS + sb * pack, al * PEER_NKEYS + (sb + 1) * pack)
            coef_new_ref[er, cs] = (gate * _gelu_tanh(act_old_ref[er, cs])).astype(BF16)

    matmuls = []
    for ts in chunks:
        matmuls += [functools.partial(score, ts), functools.partial(accumulate, ts)]
    tiles = [(al, c) for al in range(rows) for c in range(tm // LANES)]
    per = -(-len(tiles) // len(matmuls))
    for k, mm in enumerate(matmuls):
        mm()
        for al, c in tiles[k * per:(k + 1) * per]:
            gate_tile(al, c)
    act_old_ref[...] = act_new_ref[...]
    coef_old_ref[...] = coef_new_ref[...]

    @pl.when(s == pl.num_programs(1) - 1)
    def _():
        for ts in chunks:
            y_ref[ts, :] = h_ref[ts, :] + acc_ref[:, ts].T


def peer_experts(h, norm2_g, routed, u_bf, vt_bf):
    n, d = h.shape
    n_exp = u_bf.shape[0]
    tm = _pick(n, (1024, 256))
    te = 512
    nj = n_exp // te
    rows = te // PEER_NKEYS
    assert 2 * rows == SUBLANES
    assert tm % PEER_TCHUNK == 0
    s1, s2, e1, e2, tau = routed

    def blk(off):
        return lambda i, s: jnp.clip(s - off, 0, nj - 1)

    once = pl.Buffered(1)
    row = pl.BlockSpec((PEER_HEADS, SUBLANES, tm), lambda i, s: (0, blk(1)(i, s) * rows // SUBLANES, i))
    big = pl.BlockSpec((PEER_HEADS, PEER_NKEYS, tm), lambda i, s: (0, 0, i), pipeline_mode=once)
    return pl.pallas_call(
        functools.partial(_peer_kernel, tm=tm, te=te, nj=nj),
        grid=(n // tm, nj + 2),
        in_specs=[pl.BlockSpec((tm, d), lambda i, s: (i, 0), pipeline_mode=once),
                  pl.BlockSpec((1, d), lambda i, s: (0, 0)),
                  row, row, big, big,
                  pl.BlockSpec((PEER_HEADS, tm), lambda i, s: (0, i)),
                  pl.BlockSpec((te, d), lambda i, s: (blk(0)(i, s), 0)),
                  pl.BlockSpec((d, te), lambda i, s: (0, blk(2)(i, s)))],
        out_specs=pl.BlockSpec((tm, d), lambda i, s: (i, 0), pipeline_mode=once),
        out_shape=jax.ShapeDtypeStruct((n, d), F32),
        scratch_shapes=[pltpu.VMEM((d, tm), BF16),
                        pltpu.VMEM((d, tm), F32),
                        pltpu.VMEM((te, tm), F32), pltpu.VMEM((te, tm), F32),
                        pltpu.VMEM((te, tm), BF16), pltpu.VMEM((te, tm), BF16)],
        compiler_params=_params("parallel", "arbitrary"),
        name="peer_experts",
    )(h, norm2_g.reshape(1, d), s1, e1, s2, e2, tau, u_bf, vt_bf)


def _in_proj_epilogue(sw_q_g, sw_k_g, mem_q_g):
    modes = np.zeros((IN_COLS // PROJ_TN,), np.int32)
    gain = [jnp.ones((IN_CONV,), F32)]
    for g in range(SW_GROUPS):
        t0 = (COL_SW + 3 * g * SW_OUT) // PROJ_TN
        modes[t0:t0 + 2 * SW_OUT // PROJ_TN] = TILE_NORM128
        gain += [jnp.tile(sw_q_g[g], SW_HG) * (SW_HD ** -0.5), jnp.tile(sw_k_g[g], SW_HG), jnp.ones((SW_OUT,), F32)]
    modes[COL_MEMQ // PROJ_TN:COL_GATE // PROJ_TN] = TILE_NORM256
    gain += [jnp.tile(mem_q_g, MEM_H) * (MEM_HD ** -0.5), jnp.ones((3 * D_MODEL,), F32)]
    return modes, jnp.concatenate(gain)


def _layer(x, conv_prev, sw_bufs, mem_k, mem_v, w):
    b, t, d = x.shape
    n = b * t
    x2 = x.reshape(n, d)
    hcat = norm_matmul(x2, w["norm1_g"], w["w_in"], *w["in_epilogue"])
    h3 = hcat.reshape(b, t, IN_COLS)
    c_act, conv_state = conv_branch(h3, conv_prev, w["conv_w"], w["conv_b"], w["conv_ln_g"], w["conv_ln_b"])
    sw_o, sw_lse, states = [], [], []
    for g in range(SW_GROUPS):
        if sw_bufs is None:
            o, lse = sw_prompt_group(h3, g)
            keep = min(SW_WINDOWS[g], t)
            for off in (1, 2):
                col = COL_SW + (3 * g + off) * SW_OUT
                states.append(h3[:, t - keep:, col:col + SW_OUT].reshape(b, keep, SW_HG, SW_HD))
        else:
            o, lse, k_new, v_new = sw_sample_group(h3, g, sw_bufs[2 * g], sw_bufs[2 * g + 1])
            states += [k_new, v_new]
        sw_o.append(o.reshape(n, SW_OUT))
        sw_lse.append(lse.reshape(n, SW_OUT))
    if sw_bufs is None:
        mem_o = mem_attention(h3, mem_k, mem_v)
    else:
        mem_o = mem_attention_cache(h3, mem_k, mem_v)
    merged = merge_branches(hcat, c_act.reshape(n, C_CONV), sw_o, sw_lse, mem_o.reshape(n, MEM_W),
                            w["w_conv_o"], w["w_sw_o"], w["w_mem_o"])
    hres = matmul_residual(merged, w["w_out"], x2)
    q = norm_matmul(hres, w["norm2_g"], w["peer_w_q"])
    routed = peer_router(q, w["peer_k1"], w["peer_k2"])
    y = peer_experts(hres, w["norm2_g"], routed, w["peer_u"], w["peer_vt"])
    return y.reshape(b, t, d), conv_state, states


def kernel(x_prompt, x_sample, mem_prompt, state_conv, state_sw0_k, state_sw0_v, state_sw1_k, state_sw1_v, state_sw2_k, state_sw2_v, cache_mem_k, cache_mem_v, norm1_g, mem_norm_g, w_in, conv_w, conv_b, conv_ln_g, conv_ln_b, w_conv_o, sw_q_g, sw_k_g, w_sw_o, w_mem_k, w_mem_v, mem_q_g, mem_k_g, w_mem_o, w_out, norm2_g, peer_w_q, peer_k1, peer_k2, peer_u, peer_v):
    depth = w_in.shape[0]
    bp, tp, _ = x_prompt.shape
    n_mem = mem_prompt.shape[1]
    sw_states = (state_sw0_k, state_sw0_v, state_sw1_k, state_sw1_v, state_sw2_k, state_sw2_v)
    yp, ys = x_prompt, x_sample
    conv_p, conv_s, memk_p, memv_p = [], [], [], []
    sw_p = [[] for _ in range(2 * SW_GROUPS)]
    sw_s = [[] for _ in range(2 * SW_GROUPS)]
    for l in range(depth):
        w = dict(
            norm1_g=norm1_g[l], w_in=w_in[l].astype(BF16), conv_w=conv_w[l], conv_b=conv_b[l],
            conv_ln_g=conv_ln_g[l], conv_ln_b=conv_ln_b[l], w_conv_o=w_conv_o[l].astype(BF16),
            w_sw_o=w_sw_o[l].astype(BF16), w_mem_o=w_mem_o[l].astype(BF16), w_out=w_out[l].astype(BF16),
            norm2_g=norm2_g[l], peer_w_q=peer_w_q[l].astype(BF16), peer_k1=peer_k1[l].astype(BF16),
            peer_k2=peer_k2[l].astype(BF16), peer_u=peer_u[l].astype(BF16), peer_vt=peer_v[l].T.astype(BF16),
            in_epilogue=_in_proj_epilogue(sw_q_g[l], sw_k_g[l], mem_q_g[l]))
        w_kv = jnp.concatenate([w_mem_k[l], w_mem_v[l]], axis=1).astype(BF16)
        kv_modes = np.array([TILE_NORM256] * (MEM_W // PROJ_TN) + [TILE_PLAIN] * (MEM_W // PROJ_TN), np.int32)
        kv_gain = jnp.concatenate([jnp.tile(mem_k_g[l], MEM_H), jnp.ones((MEM_W,), F32)])
        kv = norm_matmul(mem_prompt.reshape(bp * n_mem, D_MODEL), mem_norm_g[l], w_kv, kv_modes, kv_gain)
        mk = kv[:, :MEM_W].reshape(bp, n_mem, MEM_W)
        mv = kv[:, MEM_W:].reshape(bp, n_mem, MEM_W)

        zeros_prev = jnp.zeros((bp, CONV_W - 1, C_CONV), F32)
        yp, cp, swp_l = _layer(yp, zeros_prev, None, mk, mv, w)
        ys, cs, sws_l = _layer(ys, state_conv[l], [s[l] for s in sw_states], cache_mem_k[l], cache_mem_v[l], w)
        conv_p.append(cp)
        conv_s.append(cs)
        memk_p.append(mk.reshape(bp, n_mem, MEM_H, MEM_HD))
        memv_p.append(mv.reshape(bp, n_mem, MEM_H, MEM_HD))
        for i in range(2 * SW_GROUPS):
            sw_p[i].append(swp_l[i])
            sw_s[i].append(sws_l[i])
    swp = [jnp.stack(a, axis=0) for a in sw_p]
    sws = [jnp.stack(a, axis=0) for a in sw_s]
    return (yp, ys, jnp.stack(conv_p, axis=0), jnp.stack(conv_s, axis=0),
            swp[0], swp[1], swp[2], swp[3], swp[4], swp[5],
            sws[0], sws[1], sws[2], sws[3], sws[4], sws[5],
            jnp.stack(memk_p, axis=0), jnp.stack(memv_p, axis=0))
```

```python
import functools
import math

import numpy as np
import jax
import jax.numpy as jnp
from jax import lax
from jax.experimental import pallas as pl
from jax.experimental.pallas import tpu as pltpu

F32 = jnp.float32
BF16 = jnp.bfloat16
EPS = 1e-6
NEG_INF = float("-inf")
LOG2_E = 1.0 / math.log(2.0)

LANES = 128
SUBLANES = 8
VMEM_LIMIT = 60 * 1024 * 1024

D_MODEL = 2048
C_CONV = 1024
CONV_W = 31
CONV_PAD = 32
SW_GROUPS = 3
SW_DILATIONS = (1, 4, 16)
SW_WINDOWS = (128, 512, 2048)
SW_SPAN = 128
SW_HG = 4
SW_HD = 128
SW_OUT = SW_HG * SW_HD
MEM_H = 4
MEM_HD = 256
MEM_W = MEM_H * MEM_HD
IN_CONV = 2 * C_CONV
IN_SW = SW_GROUPS * 3 * SW_OUT
IN_COLS = IN_CONV + IN_SW + MEM_W + 3 * D_MODEL
COL_SW = IN_CONV
COL_MEMQ = IN_CONV + IN_SW
COL_GATE = COL_MEMQ + MEM_W
PROJ_TN = 512
PEER_HEADS = 8
PEER_NKEYS = 128
PEER_DKEY = 256
PEER_TOPK = 16
PEER_TCHUNK = 256

TILE_PLAIN, TILE_NORM128, TILE_NORM256 = 0, 1, 2
_TILE_HD = {TILE_NORM128: SW_HD, TILE_NORM256: MEM_HD}


def _alibi_slopes():
    n = SW_GROUPS * SW_HG
    s = np.float32(2.0) ** (np.float32(-8.0) * np.arange(1, n + 1, dtype=np.float32) / np.float32(n))
    return s.reshape(SW_GROUPS, SW_HG)


def _rms(x, g):
    return x * lax.rsqrt(jnp.mean(x * x, axis=-1, keepdims=True) + EPS) * g


def _dot(a, b):
    return jnp.dot(a, b, preferred_element_type=F32)


def _dot_nt(a, b):
    return lax.dot_general(a, b, (((1,), (1,)), ((), ())), preferred_element_type=F32)


def _params(*sem):
    return pltpu.CompilerParams(dimension_semantics=sem, vmem_limit_bytes=VMEM_LIMIT)


def _pick(n, options):
    for o in options:
        if n % o == 0:
            return o
    raise ValueError(f"no tile in {options} divides {n}")


def _norm_matmul_kernel(mode_ref, x_ref, g_ref, w_ref, cg_ref, o_ref, xn_ref, *, tn, head_modes):
    j = pl.program_id(1)

    @pl.when(j == 0)
    def _():
        xn_ref[...] = _rms(x_ref[...], g_ref[...]).astype(BF16)

    acc = _dot(xn_ref[...], w_ref[...])
    if not head_modes:
        o_ref[...] = acc
        return
    mode = mode_ref[j]

    @pl.when(mode == TILE_PLAIN)
    def _():
        o_ref[...] = acc

    for code in head_modes:
        hd = _TILE_HD[code]

        @pl.when(mode == code)
        def _():
            for h in range(tn // hd):
                sl = slice(h * hd, (h + 1) * hd)
                o_ref[:, sl] = _rms(acc[:, sl], cg_ref[:, sl])


def norm_matmul(x, g, w_bf, tile_modes=None, col_gain=None):
    n, d = x.shape
    cols = w_bf.shape[1]
    tm = _pick(n, (1024, 256))
    tn = PROJ_TN
    nt = cols // tn
    if tile_modes is None:
        tile_modes = np.zeros((nt,), np.int32)
        col_gain = jnp.ones((cols,), F32)
    head_modes = tuple(sorted(set(int(m) for m in tile_modes) - {TILE_PLAIN}))
    grid_spec = pltpu.PrefetchScalarGridSpec(
        num_scalar_prefetch=1,
        grid=(n // tm, nt),
        in_specs=[pl.BlockSpec((tm, d), lambda i, j, m: (i, 0)),
                  pl.BlockSpec((1, d), lambda i, j, m: (0, 0)),
                  pl.BlockSpec((d, tn), lambda i, j, m: (0, j)),
                  pl.BlockSpec((1, tn), lambda i, j, m: (0, j))],
        out_specs=pl.BlockSpec((tm, tn), lambda i, j, m: (i, j)),
        scratch_shapes=[pltpu.VMEM((tm, d), BF16)])
    return pl.pallas_call(
        functools.partial(_norm_matmul_kernel, tn=tn, head_modes=head_modes),
        grid_spec=grid_spec,
        out_shape=jax.ShapeDtypeStruct((n, cols), F32),
        compiler_params=_params("parallel", "arbitrary"),
        name="norm_matmul",
    )(jnp.asarray(tile_modes, jnp.int32), x, g.reshape(1, d), w_bf, col_gain.reshape(1, cols))


def _matmul_res_kernel(a_ref, w_ref, x_ref, o_ref):
    o_ref[...] = x_ref[...] + _dot(a_ref[...], w_ref[...])


def matmul_residual(a_bf, w_bf, x):
    n, k = a_bf.shape
    cols = w_bf.shape[1]
    tm = _pick(n, (512, 256))
    tn = _pick(cols, (1024,))
    return pl.pallas_call(
        _matmul_res_kernel,
        grid=(n // tm, cols // tn),
        in_specs=[pl.BlockSpec((tm, k), lambda i, j: (i, 0)),
                  pl.BlockSpec((k, tn), lambda i, j: (0, j)),
                  pl.BlockSpec((tm, tn), lambda i, j: (i, j))],
        out_specs=pl.BlockSpec((tm, tn), lambda i, j: (i, j)),
        out_shape=jax.ShapeDtypeStruct((n, cols), F32),
        compiler_params=_params("parallel", "arbitrary"),
        name="matmul_residual",
    )(a_bf, w_bf, x)


def _conv_kernel(hc_ref, prev_ref, cw_ref, cb_ref, lg_ref, lb_ref, act_ref, st_ref, ext_ref, c_ref, *, tt):
    i = pl.program_id(1)
    hist = CONV_W - 1
    lo = CONV_PAD - hist

    @pl.when(i == 0)
    def _():
        ext_ref[lo:CONV_PAD, :] = prev_ref[0]

    @pl.when(i > 0)
    def _():
        ext_ref[lo:CONV_PAD, :] = ext_ref[tt + lo:tt + CONV_PAD, :]

    a = hc_ref[0, :, :C_CONV]
    b = hc_ref[0, :, C_CONV:]
    ext_ref[CONV_PAD:CONV_PAD + tt, :] = a * jax.nn.sigmoid(b)

    for c in range(C_CONV // LANES):
        cs = slice(c * LANES, (c + 1) * LANES)
        acc = jnp.zeros((tt, LANES), F32)
        for w in range(CONV_W):
            acc = acc + ext_ref[lo + w:lo + w + tt, cs] * cw_ref[w:w + 1, cs]
        c_ref[:, cs] = acc + cb_ref[:, cs]

    cv = c_ref[...]
    mu = jnp.mean(cv, axis=-1, keepdims=True)
    xc = cv - mu
    var = jnp.mean(xc * xc, axis=-1, keepdims=True)
    y = xc * lax.rsqrt(var + EPS) * lg_ref[...] + lb_ref[...]
    act_ref[0] = (y * jax.nn.sigmoid(y)).astype(BF16)

    @pl.when(i == pl.num_programs(1) - 1)
    def _():
        st_ref[0] = ext_ref[tt + lo:tt + CONV_PAD, :]


def conv_branch(h3, prev, conv_w, conv_b, ln_g, ln_b):
    b, t, _ = h3.shape
    tt = _pick(t, (128, 8))
    hist = CONV_W - 1
    return pl.pallas_call(
        functools.partial(_conv_kernel, tt=tt),
        grid=(b, t // tt),
        in_specs=[pl.BlockSpec((1, tt, IN_CONV), lambda bi, i: (bi, i, 0)),
                  pl.BlockSpec((1, hist, C_CONV), lambda bi, i: (bi, 0, 0)),
                  pl.BlockSpec((CONV_W, C_CONV), lambda bi, i: (0, 0)),
                  pl.BlockSpec((1, C_CONV), lambda bi, i: (0, 0)),
                  pl.BlockSpec((1, C_CONV), lambda bi, i: (0, 0)),
                  pl.BlockSpec((1, C_CONV), lambda bi, i: (0, 0))],
        out_specs=[pl.BlockSpec((1, tt, C_CONV), lambda bi, i: (bi, i, 0)),
                   pl.BlockSpec((1, hist, C_CONV), lambda bi, i: (bi, 0, 0))],
        out_shape=[jax.ShapeDtypeStruct((b, t, C_CONV), BF16),
                   jax.ShapeDtypeStruct((b, hist, C_CONV), F32)],
        scratch_shapes=[pltpu.VMEM((CONV_PAD + tt, C_CONV), F32),
                        pltpu.VMEM((tt, C_CONV), F32)],
        compiler_params=_params("parallel", "arbitrary"),
        name="conv_branch",
    )(h3, prev, conv_w, conv_b.reshape(1, C_CONV), ln_g.reshape(1, C_CONV), ln_b.reshape(1, C_CONV))


def _with_ones(v):
    return jnp.concatenate([v.astype(BF16), jnp.ones(v.shape, BF16)], axis=-1)


def _softmax_pv(s_a, s_b, v_a, v_b):
    if s_a.shape == s_b.shape:
        m = jnp.max(jnp.maximum(s_a, s_b), axis=-1, keepdims=True)
    else:
        m = jnp.maximum(jnp.max(s_a, axis=-1, keepdims=True), jnp.max(s_b, axis=-1, keepdims=True))
    p_a = jnp.exp(s_a - m).astype(BF16)
    p_b = jnp.exp(s_b - m).astype(BF16)
    hd = v_a.shape[-1]
    ol = _dot(p_a, _with_ones(v_a)) + _dot(p_b, _with_ones(v_b))
    l = ol[:, hd:]
    return ol[:, :hd] / l, m + jnp.log(l)


def _sw_prompt_kernel(slope_ref, q_ref, kc_ref, kp_ref, vc_ref, vp_ref, o_ref, lse_ref, *, r, nq):
    h = pl.program_id(1)
    n = pl.program_id(2)
    blk = SW_SPAN
    slope = slope_ref[h]
    ii = lax.broadcasted_iota(jnp.int32, (blk, blk), 0)
    cc = lax.broadcasted_iota(jnp.int32, (blk, blk), 1)
    back_c = ii - cc
    back_p = back_c + blk
    bias_c = jnp.where(back_c >= 0, -slope * (r * back_c).astype(F32), NEG_INF)
    bias_p = jnp.where(back_p <= SW_SPAN, -slope * (r * back_p).astype(F32), NEG_INF)
    bias_first = bias_p + jnp.where(n >= 1, 0.0, NEG_INF).astype(F32)

    def rows(rho, m):
        start = rho + r * blk * m
        return pl.ds(start, blk) if r == 1 else pl.ds(start, blk, stride=r)

    for rho in range(r):
        for m in range(nq):
            q = q_ref[0, rows(rho, m), :].astype(BF16)
            k_c = kc_ref[0, rows(rho, m), :]
            v_c = vc_ref[0, rows(rho, m), :]
            if m == 0:
                k_p = kp_ref[0, rows(rho, 0), :]
                v_p = vp_ref[0, rows(rho, 0), :]
                b_p = bias_first
            else:
                k_p = kc_ref[0, rows(rho, m - 1), :]
                v_p = vc_ref[0, rows(rho, m - 1), :]
                b_p = bias_p
            s_c = _dot_nt(q, k_c.astype(BF16)) + bias_c
            s_p = _dot_nt(q, k_p.astype(BF16)) + b_p
            o, lse = _softmax_pv(s_c, s_p, v_c, v_p)
            o_ref[0, rows(rho, m), :] = o
            lse_ref[0, rows(rho, m), :] = lse


def sw_prompt_group(h3, g):
    b, t, _ = h3.shape
    r = SW_DILATIONS[g]
    nq = max(1, 4 // r)
    pb = SW_SPAN * r
    ch = pb * nq
    assert t % ch == 0
    base = (COL_SW + 3 * g * SW_OUT) // SW_HD
    hpb = SW_OUT // SW_HD
    slopes = jnp.asarray(_alibi_slopes()[g])

    def cur(off):
        return pl.BlockSpec((1, ch, SW_HD), lambda bi, h, n: (bi, n, base + off * hpb + h))

    def prev(off):
        return pl.BlockSpec((1, pb, SW_HD), lambda bi, h, n: (bi, jnp.maximum(n * nq - 1, 0), base + off * hpb + h))

    out_spec = pl.BlockSpec((1, ch, SW_HD), lambda bi, h, n: (bi, n, h))
    out_sds = jax.ShapeDtypeStruct((b, t, SW_OUT), F32)
    return pl.pallas_call(
        functools.partial(_sw_prompt_kernel, r=r, nq=nq),
        grid=(b, SW_HG, t // ch),
        in_specs=[pl.BlockSpec(memory_space=pltpu.SMEM),
                  cur(0), cur(1), prev(1), cur(2), prev(2)],
        out_specs=[out_spec, out_spec],
        out_shape=[out_sds, out_sds],
        compiler_params=_params("parallel", "parallel", "arbitrary"),
        name=f"sw_prompt_g{g}",
    )(slopes, h3, h3, h3, h3, h3)


def _sw_sample_kernel(q_ref, kn_ref, vn_ref, kb_ref, vb_ref, o_ref, lse_ref, ko_ref, vo_ref,
                      *, r, buf_len, t_new, slopes):
    nh = SW_HG
    span = SW_SPAN
    shift = r.bit_length() - 1
    first = buf_len - span * r
    tq = lax.broadcasted_iota(jnp.int32, (t_new, span), 0)
    mm = lax.broadcasted_iota(jnp.int32, (t_new, span), 1)
    back_b = span + (tq >> shift) - mm
    ok_b = back_b <= span
    dist_bf = (r * back_b).astype(F32)
    cls = tq & (r - 1)
    tq2 = lax.broadcasted_iota(jnp.int32, (t_new, t_new), 0)
    tk2 = lax.broadcasted_iota(jnp.int32, (t_new, t_new), 1)
    dist_n = tq2 - tk2
    ok_n = (dist_n >= 0) & ((dist_n & (r - 1)) == 0)
    dist_nf = dist_n.astype(F32)
    keep = (buf_len - t_new) * nh
    ko_ref[pl.ds(0, keep), :] = kb_ref[pl.ds(t_new * nh, keep), :]
    vo_ref[pl.ds(0, keep), :] = vb_ref[pl.ds(t_new * nh, keep), :]
    classes = range(min(r, t_new))
    for h in range(nh):
        sl = slice(h * SW_HD, (h + 1) * SW_HD)
        q = q_ref[0, :, sl].astype(BF16)
        k_n = kn_ref[0, :, sl]
        v_n = vn_ref[0, :, sl]
        ko_ref[pl.ds(keep + h, t_new, stride=nh), :] = k_n
        vo_ref[pl.ds(keep + h, t_new, stride=nh), :] = v_n

        def class_rows(rho):
            return pl.ds((first + rho) * nh + h, span, stride=r * nh)

        s_b = None
        for rho in classes:
            s_c = _dot_nt(q, kb_ref[class_rows(rho), :].astype(BF16))
            s_b = s_c if s_b is None else jnp.where(cls == rho, s_c, s_b)
        s_b = jnp.where(ok_b, s_b - slopes[h] * dist_bf, NEG_INF)
        s_n = jnp.where(ok_n, _dot_nt(q, k_n.astype(BF16)) - slopes[h] * dist_nf, NEG_INF)
        m = jnp.maximum(jnp.max(s_b, axis=-1, keepdims=True), jnp.max(s_n, axis=-1, keepdims=True))
        p_b = jnp.exp(s_b - m)
        ol = _dot(jnp.exp(s_n - m).astype(BF16), _with_ones(v_n))
        for rho in classes:
            p_c = p_b if len(classes) == 1 else jnp.where(cls == rho, p_b, 0.0)
            ol = ol + _dot(p_c.astype(BF16), _with_ones(vb_ref[class_rows(rho), :]))
        l = ol[:, SW_HD:]
        o_ref[0, :, sl] = ol[:, :SW_HD] / l
        lse_ref[0, :, sl] = m + jnp.log(l)


def sw_sample_group(h3, g, k_buf, v_buf):
    b, t, _ = h3.shape
    buf_len = k_buf.shape[1]
    r = SW_DILATIONS[g]
    assert buf_len >= SW_SPAN * r and buf_len % r == 0 and t <= r * SW_SPAN and buf_len > t
    base = COL_SW // SW_OUT + 3 * g
    blk = (1, t, SW_OUT)
    rows = buf_len * SW_HG
    slopes = tuple(float(s) for s in _alibi_slopes()[g])
    out_spec = pl.BlockSpec(blk, lambda bi: (bi, 0, 0))
    buf_spec = pl.BlockSpec((None, rows, SW_HD), lambda bi: (bi, 0, 0))
    out_sds = jax.ShapeDtypeStruct((b, t, SW_OUT), F32)
    buf_sds = jax.ShapeDtypeStruct((b, rows, SW_HD), F32)
    k_buf = k_buf.reshape(b, rows, SW_HD)
    v_buf = v_buf.reshape(b, rows, SW_HD)
    o, lse, k_out, v_out = pl.pallas_call(
        functools.partial(_sw_sample_kernel, r=r, buf_len=buf_len, t_new=t, slopes=slopes),
        grid=(b,),
        in_specs=[pl.BlockSpec(blk, lambda bi: (bi, 0, base)),
                  pl.BlockSpec(blk, lambda bi: (bi, 0, base + 1)),
                  pl.BlockSpec(blk, lambda bi: (bi, 0, base + 2)),
                  buf_spec, buf_spec],
        out_specs=[out_spec, out_spec, buf_spec, buf_spec],
        out_shape=[out_sds, out_sds, buf_sds, buf_sds],
        compiler_params=_params("parallel"),
        name=f"sw_sample_g{g}",
    )(h3, h3, h3, k_buf, v_buf)
    shape4 = (b, buf_len, SW_HG, SW_HD)
    return o, lse, k_out.reshape(shape4), v_out.reshape(shape4)


def _attend(q, k, v):
    s = _dot_nt(q.astype(BF16), k.astype(BF16))
    m = jnp.max(s, axis=-1, keepdims=True)
    p = jnp.exp(s - m)
    p = p / jnp.sum(p, axis=-1, keepdims=True)
    return _dot(p.astype(BF16), v.astype(BF16)).astype(BF16)


def _mem_attn_kernel(q_ref, k_ref, v_ref, o_ref):
    o_ref[0] = _attend(q_ref[0], k_ref[0], v_ref[0])


def mem_attention(h3, mem_k, mem_v):
    b, t, _ = h3.shape
    n_mem = mem_k.shape[1]
    tq = _pick(t, (512,))
    qbase = COL_MEMQ // MEM_HD
    return pl.pallas_call(
        _mem_attn_kernel,
        grid=(b, t // tq, MEM_H),
        in_specs=[pl.BlockSpec((1, tq, MEM_HD), lambda bi, i, h: (bi, i, qbase + h)),
                  pl.BlockSpec((1, n_mem, MEM_HD), lambda bi, i, h: (bi, 0, h)),
                  pl.BlockSpec((1, n_mem, MEM_HD), lambda bi, i, h: (bi, 0, h))],
        out_specs=pl.BlockSpec((1, tq, MEM_HD), lambda bi, i, h: (bi, i, h)),
        out_shape=jax.ShapeDtypeStruct((b, t, MEM_W), BF16),
        compiler_params=_params("parallel", "parallel", "arbitrary"),
        name="mem_attention",
    )(h3, mem_k, mem_v)


def _mem_attn_cache_kernel(qa_ref, qb_ref, k_ref, v_ref, o_ref):
    per = PROJ_TN // MEM_HD
    for h in range(MEM_H):
        q_ref = (qa_ref, qb_ref)[h // per]
        ql = slice((h % per) * MEM_HD, (h % per + 1) * MEM_HD)
        sl = slice(h * MEM_HD, (h + 1) * MEM_HD)
        o_ref[0, :, sl] = _attend(q_ref[0, :, ql], k_ref[:, h, :], v_ref[:, h, :])


def mem_attention_cache(h3, cache_k, cache_v):
    b, t, _ = h3.shape
    n_mem = cache_k.shape[1]
    assert MEM_W == 2 * PROJ_TN
    qbase = COL_MEMQ // PROJ_TN
    cspec = pl.BlockSpec((None, n_mem, MEM_H, MEM_HD), lambda bi: (bi, 0, 0, 0))
    return pl.pallas_call(
        _mem_attn_cache_kernel,
        grid=(b,),
        in_specs=[pl.BlockSpec((1, t, PROJ_TN), lambda bi: (bi, 0, qbase)),
                  pl.BlockSpec((1, t, PROJ_TN), lambda bi: (bi, 0, qbase + 1)), cspec, cspec],
        out_specs=pl.BlockSpec((1, t, MEM_W), lambda bi: (bi, 0, 0)),
        out_shape=jax.ShapeDtypeStruct((b, t, MEM_W), BF16),
        compiler_params=_params("parallel"),
        name="mem_attention_cache",
    )(h3, h3, cache_k, cache_v)


def _merge_kernel(ca_ref, o0_ref, o1_ref, o2_ref, l0_ref, l1_ref, l2_ref, mo_ref, g0_ref, g1_ref, g2_ref,
                  wc_ref, ws_ref, wm_ref, out_ref):
    l0, l1, l2 = l0_ref[...], l1_ref[...], l2_ref[...]
    m = jnp.maximum(jnp.maximum(l0, l1), l2)
    e0, e1, e2 = jnp.exp(l0 - m), jnp.exp(l1 - m), jnp.exp(l2 - m)
    sw = (e0 * o0_ref[...] + e1 * o1_ref[...] + e2 * o2_ref[...]) / (e0 + e1 + e2)
    conv_out = _dot(ca_ref[...], wc_ref[...])
    sw_out = _dot(sw.astype(BF16), ws_ref[...])
    mem_out = _dot(mo_ref[...], wm_ref[...])
    merged = (jax.nn.sigmoid(g0_ref[...]) * conv_out + jax.nn.sigmoid(g1_ref[...]) * sw_out
              + jax.nn.sigmoid(g2_ref[...]) * mem_out)
    out_ref[...] = merged.astype(BF16)


def merge_branches(hcat, c_act, sw_o, sw_lse, mem_o, w_conv_o, w_sw_o, w_mem_o):
    n = hcat.shape[0]
    tm = _pick(n, (512, 256))
    tn = 512
    gbase = COL_GATE // tn
    gstep = D_MODEL // tn

    def row(width):
        return pl.BlockSpec((tm, width), lambda i, j: (i, 0))

    def gate(br):
        return pl.BlockSpec((tm, tn), lambda i, j: (i, gbase + br * gstep + j))

    def wcol(k):
        return pl.BlockSpec((k, tn), lambda i, j: (0, j))

    return pl.pallas_call(
        _merge_kernel,
        grid=(n // tm, D_MODEL // tn),
        in_specs=[row(C_CONV), row(SW_OUT), row(SW_OUT), row(SW_OUT), row(SW_OUT), row(SW_OUT), row(SW_OUT),
                  row(MEM_W), gate(0), gate(1), gate(2), wcol(C_CONV), wcol(SW_OUT), wcol(MEM_W)],
        out_specs=pl.BlockSpec((tm, tn), lambda i, j: (i, j)),
        out_shape=jax.ShapeDtypeStruct((n, D_MODEL), BF16),
        compiler_params=_params("parallel", "arbitrary"),
        name="merge_branches",
    )(c_act, *sw_o, *sw_lse, mem_o, hcat, hcat, hcat, w_conv_o, w_sw_o, w_mem_o)


def _sorting_network(n):
    def merge(lo, hi, r):
        step = 2 * r
        if step < hi - lo:
            yield from merge(lo, hi, step)
            yield from merge(lo + r, hi, step)
            yield from ((i, i + r) for i in range(lo + r, hi - r, step))
        else:
            yield (lo, lo + r)

    def sort(lo, hi):
        if hi > lo:
            mid = lo + (hi - lo) // 2
            yield from sort(lo, mid)
            yield from sort(mid + 1, hi)
            yield from merge(lo, hi, 1)

    return tuple(sort(0, n - 1))


def _pop_heads(lists, popped):
    return [jnp.where(popped, lists[d + 1], lists[d]) for d in range(len(lists) - 1)]


def _top_values(s, rows16, k):
    lists = [s[i * SUBLANES:(i + 1) * SUBLANES, :] for i in range(s.shape[0] // SUBLANES)]
    assert len(lists) == k
    for i, j in _sorting_network(len(lists)):
        lists[i], lists[j] = jnp.maximum(lists[i], lists[j]), jnp.minimum(lists[i], lists[j])
    vals = jnp.full(rows16.shape, NEG_INF, F32)
    for i in range(k):
        mx = jnp.max(lists[0], axis=0, keepdims=True)
        vals = jnp.where(rows16 == i, mx, vals)
        lists = _pop_heads(lists, lists[0] == mx)
    return vals


def _pair_sums(v1, v2, rows8, k):
    lo = [jnp.where((rows8 + 1) * (j + 1) <= k, v1[:SUBLANES] + v2[j:j + 1], NEG_INF) for j in range(k)]
    return lo + [v1[SUBLANES:] + v2[0:1]]


def _kth_largest(cands, k):
    lo, hi = list(cands[:-1]), cands[-1]
    tau = None
    for _ in range(k):
        tau = jnp.max(jnp.maximum(lo[0], hi), axis=0, keepdims=True)
        hi = jnp.where(hi == tau, NEG_INF, hi)
        lo = _pop_heads(lo, lo[0] == tau) if len(lo) > 1 else lo
    return tau


def _router_kernel(q_ref, k1_ref, k2_ref, s1_ref, s2_ref, tau_ref, *, tm):
    half = PEER_DKEY // 2
    rows16 = lax.broadcasted_iota(jnp.int32, (PEER_TOPK, tm), 0)
    rows8 = lax.broadcasted_iota(jnp.int32, (SUBLANES, tm), 0)
    k1 = k1_ref[...]
    k2 = k2_ref[...]

    def head(h, carry):
        c0 = pl.multiple_of(h * PEER_DKEY, PEER_DKEY)
        qa = q_ref[:, pl.ds(c0, half)].astype(BF16)
        qb = q_ref[:, pl.ds(c0 + half, half)].astype(BF16)
        s1 = _dot_nt(k1, qa)
        s2 = _dot_nt(k2, qb)
        s1 = s1 - jnp.max(s1, axis=0, keepdims=True)
        s2 = s2 - jnp.max(s2, axis=0, keepdims=True)
        v1 = _top_values(s1, rows16, PEER_TOPK)
        v2 = _top_values(s2, rows16, PEER_TOPK)
        cands = _pair_sums(v1, v2, rows8, PEER_TOPK)
        tau = _kth_largest(cands, PEER_TOPK)
        z = jnp.zeros((SUBLANES, tm), F32)
        for c in cands:
            z = z + jnp.where(c >= tau, jnp.exp(c), 0.0)
        log_z = jnp.log(jnp.sum(z, axis=0, keepdims=True))
        cands_2 = _pair_sums((v1 - log_z) * LOG2_E, v2 * LOG2_E, rows8, PEER_TOPK)
        tau_n = jnp.full((SUBLANES, tm), jnp.inf, F32)
        for c, c2 in zip(cands, cands_2):
            tau_n = jnp.minimum(tau_n, jnp.where(c >= tau, c2, jnp.inf))
        s1_ref[h] = (s1 - log_z) * LOG2_E
        s2_ref[h] = s2 * LOG2_E
        tau_ref[pl.ds(h, 1), :] = jnp.min(tau_n, axis=0, keepdims=True)
        return carry

    lax.fori_loop(0, PEER_HEADS, head, 0)


def peer_router(q, k1_bf, k2_bf):
    n = q.shape[0]
    tm = 256
    big = pl.BlockSpec((PEER_HEADS, PEER_NKEYS, tm), lambda i: (0, 0, i))
    big_sds = jax.ShapeDtypeStruct((PEER_HEADS, PEER_NKEYS, n), F32)
    return pl.pallas_call(
        functools.partial(_router_kernel, tm=tm),
        grid=(n // tm,),
        in_specs=[pl.BlockSpec((tm, PEER_HEADS * PEER_DKEY), lambda i: (i, 0)),
                  pl.BlockSpec((PEER_NKEYS, PEER_DKEY // 2), lambda i: (0, 0)),
                  pl.BlockSpec((PEER_NKEYS, PEER_DKEY // 2), lambda i: (0, 0))],
        out_specs=[big, big, pl.BlockSpec((PEER_HEADS, tm), lambda i: (0, i))],
        out_shape=[big_sds, big_sds, jax.ShapeDtypeStruct((PEER_HEADS, n), F32)],
        compiler_params=_params("parallel"),
        name="peer_router",
    )(q, k1_bf, k2_bf)


def _gelu_tanh(x):
    return 0.5 * x * (1.0 + jnp.tanh(math.sqrt(2.0 / math.pi) * (x + 0.044715 * (x * x * x))))


def _peer_kernel(h_ref, g2_ref, s1_ref, s2_ref, tau_ref, u_ref, vt_ref, y_ref,
                 hn_ref, acc_ref, act_new_ref, act_old_ref, coef_new_ref, coef_old_ref, *, tm, te, nj):
    s = pl.program_id(1)
    rows = te // PEER_NKEYS
    pack = 2 * SUBLANES
    chunks = [slice(t, t + PEER_TCHUNK) for t in range(0, tm, PEER_TCHUNK)]

    @pl.when(s == 0)
    def _():
        for ts in chunks:
            hn_ref[:, ts] = _rms(h_ref[ts, :], g2_ref[...]).T.astype(BF16)
        acc_ref[...] = jnp.zeros_like(acc_ref)
        act_old_ref[...] = jnp.zeros_like(act_old_ref)
        coef_old_ref[...] = jnp.zeros_like(coef_old_ref)

    upper = (jnp.clip(s - 1, 0, nj - 1) % 2) == 1

    def row_of(ref, h, al, cs):
        picked = jnp.where(upper, ref[h, rows + al:rows + al + 1, cs], ref[h, al:al + 1, cs])
        return jnp.broadcast_to(picked, (pack, LANES))

    def score(ts):
        act_new_ref[:, ts] = _dot(u_ref[...], hn_ref[:, ts])

    def accumulate(ts):
        acc_ref[:, ts] += _dot(vt_ref[...], coef_old_ref[:, ts])

    def gate_tile(al, c):
        cs = slice(c * LANES, (c + 1) * LANES)
        s1c, tau = [], []
        for h in range(PEER_HEADS):
            s1c.append(row_of(s1_ref, h, al, cs))
            tau.append(jnp.broadcast_to(tau_ref[h:h + 1, cs], (pack, LANES)))
        for sb in range(PEER_NKEYS // pack):
            bs = slice(sb * pack, (sb + 1) * pack)
            gate = jnp.zeros((pack, LANES), F32)
            for h in range(PEER_HEADS):
                total = s1c[h] + s2_ref[h, bs, cs]
                gate = gate + jnp.where(total >= tau[h], jnp.exp2(total), 0.0)
            er = slice(al * PEER_NKEYS + sb * pack, al * PEER_NKEYS + (sb + 1) * pack)
            coef_new_ref[er, cs] = (gate * _gelu_tanh(act_old_ref[er, cs])).astype(BF16)

    matmuls = []
    for ts in chunks:
        matmuls += [functools.partial(score, ts), functools.partial(accumulate, ts)]
    tiles = [(al, c) for al in range(rows) for c in range(tm // LANES)]
    per = -(-len(tiles) // len(matmuls))
    for k, mm in enumerate(matmuls):
        mm()
        for al, c in tiles[k * per:(k + 1) * per]:
            gate_tile(al, c)
    act_old_ref[...] = act_new_ref[...]
    coef_old_ref[...] = coef_new_ref[...]

    @pl.when(s == pl.num_programs(1) - 1)
    def _():
        for ts in chunks:
            y_ref[ts, :] = h_ref[ts, :] + acc_ref[:, ts].T


def peer_experts(h, norm2_g, routed, u_bf, vt_bf):
    n, d = h.shape
    n_exp = u_bf.shape[0]
    tm = _pick(n, (1024, 256))
    te = 512
    nj = n_exp // te
    rows = te // PEER_NKEYS
    assert 2 * rows == SUBLANES
    assert tm % PEER_TCHUNK == 0
    s1, s2, tau = routed

    def blk(off):
        return lambda i, s: jnp.clip(s - off, 0, nj - 1)

    once = pl.Buffered(1)
    row = pl.BlockSpec((PEER_HEADS, SUBLANES, tm), lambda i, s: (0, blk(1)(i, s) * rows // SUBLANES, i))
    big = pl.BlockSpec((PEER_HEADS, PEER_NKEYS, tm), lambda i, s: (0, 0, i), pipeline_mode=once)
    return pl.pallas_call(
        functools.partial(_peer_kernel, tm=tm, te=te, nj=nj),
        grid=(n // tm, nj + 2),
        in_specs=[pl.BlockSpec((tm, d), lambda i, s: (i, 0), pipeline_mode=once),
                  pl.BlockSpec((1, d), lambda i, s: (0, 0)),
                  row, big,
                  pl.BlockSpec((PEER_HEADS, tm), lambda i, s: (0, i)),
                  pl.BlockSpec((te, d), lambda i, s: (blk(0)(i, s), 0)),
                  pl.BlockSpec((d, te), lambda i, s: (0, blk(2)(i, s)))],
        out_specs=pl.BlockSpec((tm, d), lambda i, s: (i, 0), pipeline_mode=once),
        out_shape=jax.ShapeDtypeStruct((n, d), F32),
        scratch_shapes=[pltpu.VMEM((d, tm), BF16),
                        pltpu.VMEM((d, tm), F32),
                        pltpu.VMEM((te, tm), F32), pltpu.VMEM((te, tm), F32),
                        pltpu.VMEM((te, tm), BF16), pltpu.VMEM((te, tm), BF16)],
        compiler_params=_params("parallel", "arbitrary"),
        name="peer_experts",
    )(h, norm2_g.reshape(1, d), s1, s2, tau, u_bf, vt_bf)


def _in_proj_epilogue(sw_q_g, sw_k_g, mem_q_g):
    modes = np.zeros((IN_COLS // PROJ_TN,), np.int32)
    gain = [jnp.ones((IN_CONV,), F32)]
    for g in range(SW_GROUPS):
        t0 = (COL_SW + 3 * g * SW_OUT) // PROJ_TN
        modes[t0:t0 + 2 * SW_OUT // PROJ_TN] = TILE_NORM128
        gain += [jnp.tile(sw_q_g[g], SW_HG) * (SW_HD ** -0.5), jnp.tile(sw_k_g[g], SW_HG), jnp.ones((SW_OUT,), F32)]
    modes[COL_MEMQ // PROJ_TN:COL_GATE // PROJ_TN] = TILE_NORM256
    gain += [jnp.tile(mem_q_g, MEM_H) * (MEM_HD ** -0.5), jnp.ones((3 * D_MODEL,), F32)]
    return modes, jnp.concatenate(gain)


def _layer(x, conv_prev, sw_bufs, mem_k, mem_v, w):
    b, t, d = x.shape
    n = b * t
    x2 = x.reshape(n, d)
    hcat = norm_matmul(x2, w["norm1_g"], w["w_in"], *w["in_epilogue"])
    h3 = hcat.reshape(b, t, IN_COLS)
    c_act, conv_state = conv_branch(h3, conv_prev, w["conv_w"], w["conv_b"], w["conv_ln_g"], w["conv_ln_b"])
    sw_o, sw_lse, states = [], [], []
    for g in range(SW_GROUPS):
        if sw_bufs is None:
            o, lse = sw_prompt_group(h3, g)
            keep = min(SW_WINDOWS[g], t)
            for off in (1, 2):
                col = COL_SW + (3 * g + off) * SW_OUT
                states.append(h3[:, t - keep:, col:col + SW_OUT].reshape(b, keep, SW_HG, SW_HD))
        else:
            o, lse, k_new, v_new = sw_sample_group(h3, g, sw_bufs[2 * g], sw_bufs[2 * g + 1])
            states += [k_new, v_new]
        sw_o.append(o.reshape(n, SW_OUT))
        sw_lse.append(lse.reshape(n, SW_OUT))
    if sw_bufs is None:
        mem_o = mem_attention(h3, mem_k, mem_v)
    else:
        mem_o = mem_attention_cache(h3, mem_k, mem_v)
    merged = merge_branches(hcat, c_act.reshape(n, C_CONV), sw_o, sw_lse, mem_o.reshape(n, MEM_W),
                            w["w_conv_o"], w["w_sw_o"], w["w_mem_o"])
    hres = matmul_residual(merged, w["w_out"], x2)
    q = norm_matmul(hres, w["norm2_g"], w["peer_w_q"])
    routed = peer_router(q, w["peer_k1"], w["peer_k2"])
    y = peer_experts(hres, w["norm2_g"], routed, w["peer_u"], w["peer_vt"])
    return y.reshape(b, t, d), conv_state, states


def kernel(x_prompt, x_sample, mem_prompt, state_conv, state_sw0_k, state_sw0_v, state_sw1_k, state_sw1_v, state_sw2_k, state_sw2_v, cache_mem_k, cache_mem_v, norm1_g, mem_norm_g, w_in, conv_w, conv_b, conv_ln_g, conv_ln_b, w_conv_o, sw_q_g, sw_k_g, w_sw_o, w_mem_k, w_mem_v, mem_q_g, mem_k_g, w_mem_o, w_out, norm2_g, peer_w_q, peer_k1, peer_k2, peer_u, peer_v):
    depth = w_in.shape[0]
    bp, tp, _ = x_prompt.shape
    n_mem = mem_prompt.shape[1]
    sw_states = (state_sw0_k, state_sw0_v, state_sw1_k, state_sw1_v, state_sw2_k, state_sw2_v)
    yp, ys = x_prompt, x_sample
    conv_p, conv_s, memk_p, memv_p = [], [], [], []
    sw_p = [[] for _ in range(2 * SW_GROUPS)]
    sw_s = [[] for _ in range(2 * SW_GROUPS)]
    for l in range(depth):
        w = dict(
            norm1_g=norm1_g[l], w_in=w_in[l].astype(BF16), conv_w=conv_w[l], conv_b=conv_b[l],
            conv_ln_g=conv_ln_g[l], conv_ln_b=conv_ln_b[l], w_conv_o=w_conv_o[l].astype(BF16),
            w_sw_o=w_sw_o[l].astype(BF16), w_mem_o=w_mem_o[l].astype(BF16), w_out=w_out[l].astype(BF16),
            norm2_g=norm2_g[l], peer_w_q=peer_w_q[l].astype(BF16), peer_k1=peer_k1[l].astype(BF16),
            peer_k2=peer_k2[l].astype(BF16), peer_u=peer_u[l].astype(BF16), peer_vt=peer_v[l].T.astype(BF16),
            in_epilogue=_in_proj_epilogue(sw_q_g[l], sw_k_g[l], mem_q_g[l]))
        w_kv = jnp.concatenate([w_mem_k[l], w_mem_v[l]], axis=1).astype(BF16)
        kv_modes = np.array([TILE_NORM256] * (MEM_W // PROJ_TN) + [TILE_PLAIN] * (MEM_W // PROJ_TN), np.int32)
        kv_gain = jnp.concatenate([jnp.tile(mem_k_g[l], MEM_H), jnp.ones((MEM_W,), F32)])
        kv = norm_matmul(mem_prompt.reshape(bp * n_mem, D_MODEL), mem_norm_g[l], w_kv, kv_modes, kv_gain)
        mk = kv[:, :MEM_W].reshape(bp, n_mem, MEM_W)
        mv = kv[:, MEM_W:].reshape(bp, n_mem, MEM_W)

        zeros_prev = jnp.zeros((bp, CONV_W - 1, C_CONV), F32)
        yp, cp, swp_l = _layer(yp, zeros_prev, None, mk, mv, w)
        ys, cs, sws_l = _layer(ys, state_conv[l], [s[l] for s in sw_states], cache_mem_k[l], cache_mem_v[l], w)
        conv_p.append(cp)
        conv_s.append(cs)
        memk_p.append(mk.reshape(bp, n_mem, MEM_H, MEM_HD))
        memv_p.append(mv.reshape(bp, n_mem, MEM_H, MEM_HD))
        for i in range(2 * SW_GROUPS):
            sw_p[i].append(swp_l[i])
            sw_s[i].append(sws_l[i])
    swp = [jnp.stack(a, axis=0) for a in sw_p]
    sws = [jnp.stack(a, axis=0) for a in sw_s]
    return (yp, ys, jnp.stack(conv_p, axis=0), jnp.stack(conv_s, axis=0),
            swp[0], swp[1], swp[2], swp[3], swp[4], swp[5],
            sws[0], sws[1], sws[2], sws[3], sws[4], sws[5],
            jnp.stack(memk_p, axis=0), jnp.stack(memv_p, axis=0))
```

```python
import functools
import math

import numpy as np
import jax
import jax.numpy as jnp
from jax import lax
from jax.experimental import pallas as pl
from jax.experimental.pallas import tpu as pltpu

F32 = jnp.float32
BF16 = jnp.bfloat16
EPS = 1e-6
NEG_INF = float("-inf")
LOG2_E = 1.0 / math.log(2.0)

LANES = 128
SUBLANES = 8
VMEM_LIMIT = 60 * 1024 * 1024

D_MODEL = 2048
C_CONV = 1024
CONV_W = 31
CONV_PAD = 32
SW_GROUPS = 3
SW_DILATIONS = (1, 4, 16)
SW_WINDOWS = (128, 512, 2048)
SW_SPAN = 128
SW_HG = 4
SW_HD = 128
SW_OUT = SW_HG * SW_HD
MEM_H = 4
MEM_HD = 256
MEM_W = MEM_H * MEM_HD
IN_CONV = 2 * C_CONV
IN_SW = SW_GROUPS * 3 * SW_OUT
IN_COLS = IN_CONV + IN_SW + MEM_W + 3 * D_MODEL
COL_SW = IN_CONV
COL_MEMQ = IN_CONV + IN_SW
COL_GATE = COL_MEMQ + MEM_W
PROJ_TN = 512
PEER_HEADS = 8
PEER_NKEYS = 128
PEER_DKEY = 256
PEER_TOPK = 16
PEER_TCHUNK = 256

TILE_PLAIN, TILE_NORM128, TILE_NORM256 = 0, 1, 2
_TILE_HD = {TILE_NORM128: SW_HD, TILE_NORM256: MEM_HD}


def _alibi_slopes():
    n = SW_GROUPS * SW_HG
    s = np.float32(2.0) ** (np.float32(-8.0) * np.arange(1, n + 1, dtype=np.float32) / np.float32(n))
    return s.reshape(SW_GROUPS, SW_HG)


def _rms(x, g):
    return x * lax.rsqrt(jnp.mean(x * x, axis=-1, keepdims=True) + EPS) * g


def _dot(a, b):
    return jnp.dot(a, b, preferred_element_type=F32)


def _dot_nt(a, b):
    return lax.dot_general(a, b, (((1,), (1,)), ((), ())), preferred_element_type=F32)


def _params(*sem):
    return pltpu.CompilerParams(dimension_semantics=sem, vmem_limit_bytes=VMEM_LIMIT)


def _pick(n, options):
    for o in options:
        if n % o == 0:
            return o
    raise ValueError(f"no tile in {options} divides {n}")


def _norm_matmul_kernel(mode_ref, x_ref, g_ref, w_ref, cg_ref, o_ref, xn_ref, *, tn, head_modes):
    j = pl.program_id(1)

    @pl.when(j == 0)
    def _():
        xn_ref[...] = _rms(x_ref[...], g_ref[...]).astype(BF16)

    acc = _dot(xn_ref[...], w_ref[...])
    if not head_modes:
        o_ref[...] = acc
        return
    mode = mode_ref[j]

    @pl.when(mode == TILE_PLAIN)
    def _():
        o_ref[...] = acc

    for code in head_modes:
        hd = _TILE_HD[code]

        @pl.when(mode == code)
        def _():
            for h in range(tn // hd):
                sl = slice(h * hd, (h + 1) * hd)
                o_ref[:, sl] = _rms(acc[:, sl], cg_ref[:, sl])


def norm_matmul(x, g, w_bf, tile_modes=None, col_gain=None):
    n, d = x.shape
    cols = w_bf.shape[1]
    tm = _pick(n, (1024, 256))
    tn = PROJ_TN
    nt = cols // tn
    if tile_modes is None:
        tile_modes = np.zeros((nt,), np.int32)
        col_gain = jnp.ones((cols,), F32)
    head_modes = tuple(sorted(set(int(m) for m in tile_modes) - {TILE_PLAIN}))
    grid_spec = pltpu.PrefetchScalarGridSpec(
        num_scalar_prefetch=1,
        grid=(n // tm, nt),
        in_specs=[pl.BlockSpec((tm, d), lambda i, j, m: (i, 0)),
                  pl.BlockSpec((1, d), lambda i, j, m: (0, 0)),
                  pl.BlockSpec((d, tn), lambda i, j, m: (0, j)),
                  pl.BlockSpec((1, tn), lambda i, j, m: (0, j))],
        out_specs=pl.BlockSpec((tm, tn), lambda i, j, m: (i, j)),
        scratch_shapes=[pltpu.VMEM((tm, d), BF16)])
    return pl.pallas_call(
        functools.partial(_norm_matmul_kernel, tn=tn, head_modes=head_modes),
        grid_spec=grid_spec,
        out_shape=jax.ShapeDtypeStruct((n, cols), F32),
        compiler_params=_params("parallel", "arbitrary"),
        name="norm_matmul",
    )(jnp.asarray(tile_modes, jnp.int32), x, g.reshape(1, d), w_bf, col_gain.reshape(1, cols))


def _matmul_res_kernel(a_ref, w_ref, x_ref, o_ref):
    o_ref[...] = x_ref[...] + _dot(a_ref[...], w_ref[...])


def matmul_residual(a_bf, w_bf, x):
    n, k = a_bf.shape
    cols = w_bf.shape[1]
    tm = _pick(n, (512, 256))
    tn = _pick(cols, (1024,))
    return pl.pallas_call(
        _matmul_res_kernel,
        grid=(n // tm, cols // tn),
        in_specs=[pl.BlockSpec((tm, k), lambda i, j: (i, 0)),
                  pl.BlockSpec((k, tn), lambda i, j: (0, j)),
                  pl.BlockSpec((tm, tn), lambda i, j: (i, j))],
        out_specs=pl.BlockSpec((tm, tn), lambda i, j: (i, j)),
        out_shape=jax.ShapeDtypeStruct((n, cols), F32),
        compiler_params=_params("parallel", "arbitrary"),
        name="matmul_residual",
    )(a_bf, w_bf, x)


def _conv_kernel(hc_ref, prev_ref, cw_ref, cb_ref, lg_ref, lb_ref, act_ref, st_ref, ext_ref, c_ref, *, tt):
    i = pl.program_id(1)
    hist = CONV_W - 1
    lo = CONV_PAD - hist

    @pl.when(i == 0)
    def _():
        ext_ref[lo:CONV_PAD, :] = prev_ref[0]

    @pl.when(i > 0)
    def _():
        ext_ref[lo:CONV_PAD, :] = ext_ref[tt + lo:tt + CONV_PAD, :]

    a = hc_ref[0, :, :C_CONV]
    b = hc_ref[0, :, C_CONV:]
    ext_ref[CONV_PAD:CONV_PAD + tt, :] = a * jax.nn.sigmoid(b)

    for c in range(C_CONV // LANES):
        cs = slice(c * LANES, (c + 1) * LANES)
        acc = jnp.zeros((tt, LANES), F32)
        for w in range(CONV_W):
            acc = acc + ext_ref[lo + w:lo + w + tt, cs] * cw_ref[w:w + 1, cs]
        c_ref[:, cs] = acc + cb_ref[:, cs]

    cv = c_ref[...]
    mu = jnp.mean(cv, axis=-1, keepdims=True)
    xc = cv - mu
    var = jnp.mean(xc * xc, axis=-1, keepdims=True)
    y = xc * lax.rsqrt(var + EPS) * lg_ref[...] + lb_ref[...]
    act_ref[0] = (y * jax.nn.sigmoid(y)).astype(BF16)

    @pl.when(i == pl.num_programs(1) - 1)
    def _():
        st_ref[0] = ext_ref[tt + lo:tt + CONV_PAD, :]


def conv_branch(h3, prev, conv_w, conv_b, ln_g, ln_b):
    b, t, _ = h3.shape
    tt = _pick(t, (128, 8))
    hist = CONV_W - 1
    return pl.pallas_call(
        functools.partial(_conv_kernel, tt=tt),
        grid=(b, t // tt),
        in_specs=[pl.BlockSpec((1, tt, IN_CONV), lambda bi, i: (bi, i, 0)),
                  pl.BlockSpec((1, hist, C_CONV), lambda bi, i: (bi, 0, 0)),
                  pl.BlockSpec((CONV_W, C_CONV), lambda bi, i: (0, 0)),
                  pl.BlockSpec((1, C_CONV), lambda bi, i: (0, 0)),
                  pl.BlockSpec((1, C_CONV), lambda bi, i: (0, 0)),
                  pl.BlockSpec((1, C_CONV), lambda bi, i: (0, 0))],
        out_specs=[pl.BlockSpec((1, tt, C_CONV), lambda bi, i: (bi, i, 0)),
                   pl.BlockSpec((1, hist, C_CONV), lambda bi, i: (bi, 0, 0))],
        out_shape=[jax.ShapeDtypeStruct((b, t, C_CONV), BF16),
                   jax.ShapeDtypeStruct((b, hist, C_CONV), F32)],
        scratch_shapes=[pltpu.VMEM((CONV_PAD + tt, C_CONV), F32),
                        pltpu.VMEM((tt, C_CONV), F32)],
        compiler_params=_params("parallel", "arbitrary"),
        name="conv_branch",
    )(h3, prev, conv_w, conv_b.reshape(1, C_CONV), ln_g.reshape(1, C_CONV), ln_b.reshape(1, C_CONV))


def _with_ones(v):
    return jnp.concatenate([v.astype(BF16), jnp.ones(v.shape, BF16)], axis=-1)


def _softmax_pv(s_a, s_b, v_a, v_b):
    if s_a.shape == s_b.shape:
        m = jnp.max(jnp.maximum(s_a, s_b), axis=-1, keepdims=True)
    else:
        m = jnp.maximum(jnp.max(s_a, axis=-1, keepdims=True), jnp.max(s_b, axis=-1, keepdims=True))
    p_a = jnp.exp(s_a - m).astype(BF16)
    p_b = jnp.exp(s_b - m).astype(BF16)
    hd = v_a.shape[-1]
    ol = _dot(p_a, _with_ones(v_a)) + _dot(p_b, _with_ones(v_b))
    l = ol[:, hd:]
    return ol[:, :hd] / l, m + jnp.log(l)


def _sw_prompt_kernel(slope_ref, q_ref, kc_ref, kp_ref, vc_ref, vp_ref, o_ref, lse_ref, *, r, nq):
    h = pl.program_id(1)
    n = pl.program_id(2)
    blk = SW_SPAN
    slope = slope_ref[h]
    ii = lax.broadcasted_iota(jnp.int32, (blk, blk), 0)
    cc = lax.broadcasted_iota(jnp.int32, (blk, blk), 1)
    back_c = ii - cc
    back_p = back_c + blk
    bias_c = jnp.where(back_c >= 0, -slope * (r * back_c).astype(F32), NEG_INF)
    bias_p = jnp.where(back_p <= SW_SPAN, -slope * (r * back_p).astype(F32), NEG_INF)
    bias_first = bias_p + jnp.where(n >= 1, 0.0, NEG_INF).astype(F32)

    def rows(rho, m):
        start = rho + r * blk * m
        return pl.ds(start, blk) if r == 1 else pl.ds(start, blk, stride=r)

    for rho in range(r):
        for m in range(nq):
            q = q_ref[0, rows(rho, m), :].astype(BF16)
            k_c = kc_ref[0, rows(rho, m), :]
            v_c = vc_ref[0, rows(rho, m), :]
            if m == 0:
                k_p = kp_ref[0, rows(rho, 0), :]
                v_p = vp_ref[0, rows(rho, 0), :]
                b_p = bias_first
            else:
                k_p = kc_ref[0, rows(rho, m - 1), :]
                v_p = vc_ref[0, rows(rho, m - 1), :]
                b_p = bias_p
            s_c = _dot_nt(q, k_c.astype(BF16)) + bias_c
            s_p = _dot_nt(q, k_p.astype(BF16)) + b_p
            o, lse = _softmax_pv(s_c, s_p, v_c, v_p)
            o_ref[0, rows(rho, m), :] = o
            lse_ref[0, rows(rho, m), :] = lse


def sw_prompt_group(h3, g):
    b, t, _ = h3.shape
    r = SW_DILATIONS[g]
    nq = max(1, 4 // r)
    pb = SW_SPAN * r
    ch = pb * nq
    assert t % ch == 0
    base = (COL_SW + 3 * g * SW_OUT) // SW_HD
    hpb = SW_OUT // SW_HD
    slopes = jnp.asarray(_alibi_slopes()[g])

    def cur(off):
        return pl.BlockSpec((1, ch, SW_HD), lambda bi, h, n: (bi, n, base + off * hpb + h))

    def prev(off):
        return pl.BlockSpec((1, pb, SW_HD), lambda bi, h, n: (bi, jnp.maximum(n * nq - 1, 0), base + off * hpb + h))

    out_spec = pl.BlockSpec((1, ch, SW_HD), lambda bi, h, n: (bi, n, h))
    out_sds = jax.ShapeDtypeStruct((b, t, SW_OUT), F32)
    return pl.pallas_call(
        functools.partial(_sw_prompt_kernel, r=r, nq=nq),
        grid=(b, SW_HG, t // ch),
        in_specs=[pl.BlockSpec(memory_space=pltpu.SMEM),
                  cur(0), cur(1), prev(1), cur(2), prev(2)],
        out_specs=[out_spec, out_spec],
        out_shape=[out_sds, out_sds],
        compiler_params=_params("parallel", "parallel", "arbitrary"),
        name=f"sw_prompt_g{g}",
    )(slopes, h3, h3, h3, h3, h3)


def _sw_sample_kernel(q_ref, kn_ref, vn_ref, kb_ref, vb_ref, o_ref, lse_ref, ko_ref, vo_ref,
                      *, r, buf_len, t_new, slopes):
    nh = SW_HG
    span = SW_SPAN
    shift = r.bit_length() - 1
    first = buf_len - span * r
    tq = lax.broadcasted_iota(jnp.int32, (t_new, span), 0)
    mm = lax.broadcasted_iota(jnp.int32, (t_new, span), 1)
    back_b = span + (tq >> shift) - mm
    ok_b = back_b <= span
    dist_bf = (r * back_b).astype(F32)
    cls = tq & (r - 1)
    tq2 = lax.broadcasted_iota(jnp.int32, (t_new, t_new), 0)
    tk2 = lax.broadcasted_iota(jnp.int32, (t_new, t_new), 1)
    dist_n = tq2 - tk2
    ok_n = (dist_n >= 0) & ((dist_n & (r - 1)) == 0)
    dist_nf = dist_n.astype(F32)
    keep = (buf_len - t_new) * nh
    ko_ref[pl.ds(0, keep), :] = kb_ref[pl.ds(t_new * nh, keep), :]
    vo_ref[pl.ds(0, keep), :] = vb_ref[pl.ds(t_new * nh, keep), :]
    classes = range(min(r, t_new))
    for h in range(nh):
        sl = slice(h * SW_HD, (h + 1) * SW_HD)
        q = q_ref[0, :, sl].astype(BF16)
        k_n = kn_ref[0, :, sl]
        v_n = vn_ref[0, :, sl]
        ko_ref[pl.ds(keep + h, t_new, stride=nh), :] = k_n
        vo_ref[pl.ds(keep + h, t_new, stride=nh), :] = v_n

        def class_rows(rho):
            return pl.ds((first + rho) * nh + h, span, stride=r * nh)

        s_b = None
        for rho in classes:
            s_c = _dot_nt(q, kb_ref[class_rows(rho), :].astype(BF16))
            s_b = s_c if s_b is None else jnp.where(cls == rho, s_c, s_b)
        s_b = jnp.where(ok_b, s_b - slopes[h] * dist_bf, NEG_INF)
        s_n = jnp.where(ok_n, _dot_nt(q, k_n.astype(BF16)) - slopes[h] * dist_nf, NEG_INF)
        m = jnp.maximum(jnp.max(s_b, axis=-1, keepdims=True), jnp.max(s_n, axis=-1, keepdims=True))
        p_b = jnp.exp(s_b - m)
        ol = _dot(jnp.exp(s_n - m).astype(BF16), _with_ones(v_n))
        for rho in classes:
            p_c = p_b if len(classes) == 1 else jnp.where(cls == rho, p_b, 0.0)
            ol = ol + _dot(p_c.astype(BF16), _with_ones(vb_ref[class_rows(rho), :]))
        l = ol[:, SW_HD:]
        o_ref[0, :, sl] = ol[:, :SW_HD] / l
        lse_ref[0, :, sl] = m + jnp.log(l)


def sw_sample_group(h3, g, k_buf, v_buf):
    b, t, _ = h3.shape
    buf_len = k_buf.shape[1]
    r = SW_DILATIONS[g]
    assert buf_len >= SW_SPAN * r and buf_len % r == 0 and t <= r * SW_SPAN and buf_len > t
    base = COL_SW // SW_OUT + 3 * g
    blk = (1, t, SW_OUT)
    rows = buf_len * SW_HG
    slopes = tuple(float(s) for s in _alibi_slopes()[g])
    out_spec = pl.BlockSpec(blk, lambda bi: (bi, 0, 0))
    buf_spec = pl.BlockSpec((None, rows, SW_HD), lambda bi: (bi, 0, 0))
    out_sds = jax.ShapeDtypeStruct((b, t, SW_OUT), F32)
    buf_sds = jax.ShapeDtypeStruct((b, rows, SW_HD), F32)
    k_buf = k_buf.reshape(b, rows, SW_HD)
    v_buf = v_buf.reshape(b, rows, SW_HD)
    o, lse, k_out, v_out = pl.pallas_call(
        functools.partial(_sw_sample_kernel, r=r, buf_len=buf_len, t_new=t, slopes=slopes),
        grid=(b,),
        in_specs=[pl.BlockSpec(blk, lambda bi: (bi, 0, base)),
                  pl.BlockSpec(blk, lambda bi: (bi, 0, base + 1)),
                  pl.BlockSpec(blk, lambda bi: (bi, 0, base + 2)),
                  buf_spec, buf_spec],
        out_specs=[out_spec, out_spec, buf_spec, buf_spec],
        out_shape=[out_sds, out_sds, buf_sds, buf_sds],
        compiler_params=_params("parallel"),
        name=f"sw_sample_g{g}",
    )(h3, h3, h3, k_buf, v_buf)
    shape4 = (b, buf_len, SW_HG, SW_HD)
    return o, lse, k_out.reshape(shape4), v_out.reshape(shape4)


def _attend(q, k, v):
    s = _dot_nt(q.astype(BF16), k.astype(BF16))
    m = jnp.max(s, axis=-1, keepdims=True)
    p = jnp.exp(s - m)
    p = p / jnp.sum(p, axis=-1, keepdims=True)
    return _dot(p.astype(BF16), v.astype(BF16)).astype(BF16)


def _mem_attn_kernel(q_ref, k_ref, v_ref, o_ref):
    o_ref[0] = _attend(q_ref[0], k_ref[0], v_ref[0])


def mem_attention(h3, mem_k, mem_v):
    b, t, _ = h3.shape
    n_mem = mem_k.shape[1]
    tq = _pick(t, (512,))
    qbase = COL_MEMQ // MEM_HD
    return pl.pallas_call(
        _mem_attn_kernel,
        grid=(b, t // tq, MEM_H),
        in_specs=[pl.BlockSpec((1, tq, MEM_HD), lambda bi, i, h: (bi, i, qbase + h)),
                  pl.BlockSpec((1, n_mem, MEM_HD), lambda bi, i, h: (bi, 0, h)),
                  pl.BlockSpec((1, n_mem, MEM_HD), lambda bi, i, h: (bi, 0, h))],
        out_specs=pl.BlockSpec((1, tq, MEM_HD), lambda bi, i, h: (bi, i, h)),
        out_shape=jax.ShapeDtypeStruct((b, t, MEM_W), BF16),
        compiler_params=_params("parallel", "parallel", "arbitrary"),
        name="mem_attention",
    )(h3, mem_k, mem_v)


def _mem_attn_cache_kernel(qa_ref, qb_ref, k_ref, v_ref, o_ref):
    per = PROJ_TN // MEM_HD
    for h in range(MEM_H):
        q_ref = (qa_ref, qb_ref)[h // per]
        ql = slice((h % per) * MEM_HD, (h % per + 1) * MEM_HD)
        sl = slice(h * MEM_HD, (h + 1) * MEM_HD)
        o_ref[0, :, sl] = _attend(q_ref[0, :, ql], k_ref[:, h, :], v_ref[:, h, :])


def mem_attention_cache(h3, cache_k, cache_v):
    b, t, _ = h3.shape
    n_mem = cache_k.shape[1]
    assert MEM_W == 2 * PROJ_TN
    qbase = COL_MEMQ // PROJ_TN
    cspec = pl.BlockSpec((None, n_mem, MEM_H, MEM_HD), lambda bi: (bi, 0, 0, 0))
    return pl.pallas_call(
        _mem_attn_cache_kernel,
        grid=(b,),
        in_specs=[pl.BlockSpec((1, t, PROJ_TN), lambda bi: (bi, 0, qbase)),
                  pl.BlockSpec((1, t, PROJ_TN), lambda bi: (bi, 0, qbase + 1)), cspec, cspec],
        out_specs=pl.BlockSpec((1, t, MEM_W), lambda bi: (bi, 0, 0)),
        out_shape=jax.ShapeDtypeStruct((b, t, MEM_W), BF16),
        compiler_params=_params("parallel"),
        name="mem_attention_cache",
    )(h3, h3, cache_k, cache_v)


def _merge_kernel(*refs, n_gate):
    ca_ref, o0_ref, o1_ref, o2_ref, l0_ref, l1_ref, l2_ref, mo_ref = refs[:8]
    gate_refs = refs[8:8 + 3 * n_gate]
    x_ref, wc_ref, ws_ref, wm_ref, wo_ref, h_ref, merged_ref = refs[8 + 3 * n_gate:]
    l0, l1, l2 = l0_ref[...], l1_ref[...], l2_ref[...]
    m = jnp.maximum(jnp.maximum(l0, l1), l2)
    e0, e1, e2 = jnp.exp(l0 - m), jnp.exp(l1 - m), jnp.exp(l2 - m)
    sw = ((e0 * o0_ref[...] + e1 * o1_ref[...] + e2 * o2_ref[...]) / (e0 + e1 + e2)).astype(BF16)
    tn = D_MODEL // n_gate
    for j in range(n_gate):
        cs = slice(j * tn, (j + 1) * tn)
        merged = (jax.nn.sigmoid(gate_refs[j][...]) * _dot(ca_ref[...], wc_ref[:, cs])
                  + jax.nn.sigmoid(gate_refs[n_gate + j][...]) * _dot(sw, ws_ref[:, cs])
                  + jax.nn.sigmoid(gate_refs[2 * n_gate + j][...]) * _dot(mo_ref[...], wm_ref[:, cs]))
        merged_ref[:, cs] = merged.astype(BF16)
    h_ref[...] = x_ref[...] + _dot(merged_ref[...], wo_ref[...])


def merge_project(hcat, x, c_act, sw_o, sw_lse, mem_o, w_conv_o, w_sw_o, w_mem_o, w_out):
    n = hcat.shape[0]
    tm = 256
    tn = PROJ_TN
    n_gate = D_MODEL // tn
    gbase = COL_GATE // tn
    once = pl.Buffered(1)

    def row(width):
        return pl.BlockSpec((tm, width), lambda i: (i, 0))

    def gate(k):
        return pl.BlockSpec((tm, tn), lambda i: (i, gbase + k))

    def weight(w):
        return pl.BlockSpec(w.shape, lambda i: (0, 0), pipeline_mode=once)

    weights = (w_conv_o, w_sw_o, w_mem_o, w_out)
    return pl.pallas_call(
        functools.partial(_merge_kernel, n_gate=n_gate),
        grid=(n // tm,),
        in_specs=[row(C_CONV)] + [row(SW_OUT)] * 6 + [row(MEM_W)] + [gate(k) for k in range(3 * n_gate)]
                 + [row(D_MODEL)] + [weight(w) for w in weights],
        out_specs=row(D_MODEL),
        out_shape=jax.ShapeDtypeStruct((n, D_MODEL), F32),
        scratch_shapes=[pltpu.VMEM((tm, D_MODEL), BF16)],
        compiler_params=_params("parallel"),
        name="merge_project",
    )(c_act, *sw_o, *sw_lse, mem_o, *([hcat] * (3 * n_gate)), x, *weights)


def _sorting_network(n):
    def merge(lo, hi, r):
        step = 2 * r
        if step < hi - lo:
            yield from merge(lo, hi, step)
            yield from merge(lo + r, hi, step)
            yield from ((i, i + r) for i in range(lo + r, hi - r, step))
        else:
            yield (lo, lo + r)

    def sort(lo, hi):
        if hi > lo:
            mid = lo + (hi - lo) // 2
            yield from sort(lo, mid)
            yield from sort(mid + 1, hi)
            yield from merge(lo, hi, 1)

    return tuple(sort(0, n - 1))


def _pop_heads(lists, popped):
    return [jnp.where(popped, lists[d + 1], lists[d]) for d in range(len(lists) - 1)]


def _top_values(s, rows16, k):
    lists = [s[i * SUBLANES:(i + 1) * SUBLANES, :] for i in range(s.shape[0] // SUBLANES)]
    assert len(lists) == k
    for i, j in _sorting_network(len(lists)):
        lists[i], lists[j] = jnp.maximum(lists[i], lists[j]), jnp.minimum(lists[i], lists[j])
    vals = jnp.full(rows16.shape, NEG_INF, F32)
    for i in range(k):
        mx = jnp.max(lists[0], axis=0, keepdims=True)
        vals = jnp.where(rows16 == i, mx, vals)
        lists = _pop_heads(lists, lists[0] == mx)
    return vals


def _pair_sums(v1, v2, rows8, k):
    lo = [jnp.where((rows8 + 1) * (j + 1) <= k, v1[:SUBLANES] + v2[j:j + 1], NEG_INF) for j in range(k)]
    return lo + [v1[SUBLANES:] + v2[0:1]]


def _kth_largest(cands, k):
    lo, hi = list(cands[:-1]), cands[-1]
    tau = None
    for _ in range(k):
        tau = jnp.max(jnp.maximum(lo[0], hi), axis=0, keepdims=True)
        hi = jnp.where(hi == tau, NEG_INF, hi)
        lo = _pop_heads(lo, lo[0] == tau) if len(lo) > 1 else lo
    return tau


def _router_kernel(q_ref, k1_ref, k2_ref, s1_ref, s2_ref, tau_ref, *, tm):
    half = PEER_DKEY // 2
    rows16 = lax.broadcasted_iota(jnp.int32, (PEER_TOPK, tm), 0)
    rows8 = lax.broadcasted_iota(jnp.int32, (SUBLANES, tm), 0)
    k1 = k1_ref[...]
    k2 = k2_ref[...]

    def head(h, carry):
        c0 = pl.multiple_of(h * PEER_DKEY, PEER_DKEY)
        qa = q_ref[:, pl.ds(c0, half)].astype(BF16)
        qb = q_ref[:, pl.ds(c0 + half, half)].astype(BF16)
        s1 = _dot_nt(k1, qa)
        s2 = _dot_nt(k2, qb)
        s1 = s1 - jnp.max(s1, axis=0, keepdims=True)
        s2 = s2 - jnp.max(s2, axis=0, keepdims=True)
        v1 = _top_values(s1, rows16, PEER_TOPK)
        v2 = _top_values(s2, rows16, PEER_TOPK)
        cands = _pair_sums(v1, v2, rows8, PEER_TOPK)
        tau = _kth_largest(cands, PEER_TOPK)
        z = jnp.zeros((SUBLANES, tm), F32)
        for c in cands:
            z = z + jnp.where(c >= tau, jnp.exp(c), 0.0)
        log_z = jnp.log(jnp.sum(z, axis=0, keepdims=True))
        cands_2 = _pair_sums((v1 - log_z) * LOG2_E, v2 * LOG2_E, rows8, PEER_TOPK)
        tau_n = jnp.full((SUBLANES, tm), jnp.inf, F32)
        for c, c2 in zip(cands, cands_2):
            tau_n = jnp.minimum(tau_n, jnp.where(c >= tau, c2, jnp.inf))
        s1_ref[h] = (s1 - log_z) * LOG2_E
        s2_ref[h] = s2 * LOG2_E
        tau_ref[pl.ds(h, 1), :] = jnp.min(tau_n, axis=0, keepdims=True)
        return carry

    lax.fori_loop(0, PEER_HEADS, head, 0)


def peer_router(q, k1_bf, k2_bf):
    n = q.shape[0]
    tm = 256
    big = pl.BlockSpec((PEER_HEADS, PEER_NKEYS, tm), lambda i: (0, 0, i))
    big_sds = jax.ShapeDtypeStruct((PEER_HEADS, PEER_NKEYS, n), F32)
    return pl.pallas_call(
        functools.partial(_router_kernel, tm=tm),
        grid=(n // tm,),
        in_specs=[pl.BlockSpec((tm, PEER_HEADS * PEER_DKEY), lambda i: (i, 0)),
                  pl.BlockSpec((PEER_NKEYS, PEER_DKEY // 2), lambda i: (0, 0)),
                  pl.BlockSpec((PEER_NKEYS, PEER_DKEY // 2), lambda i: (0, 0))],
        out_specs=[big, big, pl.BlockSpec((PEER_HEADS, tm), lambda i: (0, i))],
        out_shape=[big_sds, big_sds, jax.ShapeDtypeStruct((PEER_HEADS, n), F32)],
        compiler_params=_params("parallel"),
        name="peer_router",
    )(q, k1_bf, k2_bf)


def _gelu_tanh(x):
    c = 2.0 * math.sqrt(2.0 / math.pi) * LOG2_E
    return x / (1.0 + jnp.exp2(x * (-c - (0.044715 * c) * (x * x))))


def _peer_kernel(h_ref, g2_ref, s1_ref, s2_ref, tau_ref, u_ref, vt_ref, y_ref,
                 hn_ref, acc_ref, act_new_ref, act_old_ref, coef_new_ref, coef_old_ref, *, tm, te, nj):
    s = pl.program_id(1)
    rows = te // PEER_NKEYS
    pack = 2 * SUBLANES
    chunks = [slice(t, t + PEER_TCHUNK) for t in range(0, tm, PEER_TCHUNK)]

    @pl.when(s == 0)
    def _():
        for ts in chunks:
            hn_ref[:, ts] = _rms(h_ref[ts, :], g2_ref[...]).T.astype(BF16)
        acc_ref[...] = jnp.zeros_like(acc_ref)
        act_old_ref[...] = jnp.zeros_like(act_old_ref)
        coef_old_ref[...] = jnp.zeros_like(coef_old_ref)

    upper = (jnp.clip(s - 1, 0, nj - 1) % 2) == 1

    def row_of(ref, h, al, cs):
        picked = jnp.where(upper, ref[h, rows + al:rows + al + 1, cs], ref[h, al:al + 1, cs])
        return jnp.broadcast_to(picked, (pack, LANES))

    def score(ts):
        act_new_ref[:, ts] = _dot(u_ref[...], hn_ref[:, ts])

    def accumulate(ts):
        acc_ref[:, ts] += _dot(vt_ref[...], coef_old_ref[:, ts])

    def gate_tile(al, c):
        cs = slice(c * LANES, (c + 1) * LANES)
        s1c, tau = [], []
        for h in range(PEER_HEADS):
            s1c.append(row_of(s1_ref, h, al, cs))
            tau.append(jnp.broadcast_to(tau_ref[h:h + 1, cs], (pack, LANES)))
        for sb in range(PEER_NKEYS // pack):
            bs = slice(sb * pack, (sb + 1) * pack)
            gate = jnp.zeros((pack, LANES), F32)
            for h in range(PEER_HEADS):
                total = s1c[h] + s2_ref[h, bs, cs]
                gate = gate + jnp.where(total >= tau[h], jnp.exp2(total), 0.0)
            er = slice(al * PEER_NKEYS + sb * pack, al * PEER_NKEYS + (sb + 1) * pack)
            coef_new_ref[er, cs] = (gate * _gelu_tanh(act_old_ref[er, cs])).astype(BF16)

    matmuls = []
    for ts in chunks:
        matmuls += [functools.partial(score, ts), functools.partial(accumulate, ts)]
    tiles = [(al, c) for al in range(rows) for c in range(tm // LANES)]
    per = -(-len(tiles) // len(matmuls))
    for k, mm in enumerate(matmuls):
        mm()
        for al, c in tiles[k * per:(k + 1) * per]:
            gate_tile(al, c)
    act_old_ref[...] = act_new_ref[...]
    coef_old_ref[...] = coef_new_ref[...]

    @pl.when(s == pl.num_programs(1) - 1)
    def _():
        for ts in chunks:
            y_ref[ts, :] = h_ref[ts, :] + acc_ref[:, ts].T


def peer_experts(h, norm2_g, routed, u_bf, vt_bf):
    n, d = h.shape
    n_exp = u_bf.shape[0]
    tm = _pick(n, (1024, 256))
    te = 512
    nj = n_exp // te
    rows = te // PEER_NKEYS
    assert 2 * rows == SUBLANES
    assert tm % PEER_TCHUNK == 0
    s1, s2, tau = routed

    def blk(off):
        return lambda i, s: jnp.clip(s - off, 0, nj - 1)

    once = pl.Buffered(1)
    row = pl.BlockSpec((PEER_HEADS, SUBLANES, tm), lambda i, s: (0, blk(1)(i, s) * rows // SUBLANES, i))
    big = pl.BlockSpec((PEER_HEADS, PEER_NKEYS, tm), lambda i, s: (0, 0, i), pipeline_mode=once)
    return pl.pallas_call(
        functools.partial(_peer_kernel, tm=tm, te=te, nj=nj),
        grid=(n // tm, nj + 2),
        in_specs=[pl.BlockSpec((tm, d), lambda i, s: (i, 0), pipeline_mode=once),
                  pl.BlockSpec((1, d), lambda i, s: (0, 0)),
                  row, big,
                  pl.BlockSpec((PEER_HEADS, tm), lambda i, s: (0, i)),
                  pl.BlockSpec((te, d), lambda i, s: (blk(0)(i, s), 0)),
                  pl.BlockSpec((d, te), lambda i, s: (0, blk(2)(i, s)))],
        out_specs=pl.BlockSpec((tm, d), lambda i, s: (i, 0), pipeline_mode=once),
        out_shape=jax.ShapeDtypeStruct((n, d), F32),
        scratch_shapes=[pltpu.VMEM((d, tm), BF16),
                        pltpu.VMEM((d, tm), F32),
                        pltpu.VMEM((te, tm), F32), pltpu.VMEM((te, tm), F32),
                        pltpu.VMEM((te, tm), BF16), pltpu.VMEM((te, tm), BF16)],
        compiler_params=_params("parallel", "arbitrary"),
        name="peer_experts",
    )(h, norm2_g.reshape(1, d), s1, s2, tau, u_bf, vt_bf)


def _in_proj_epilogue(sw_q_g, sw_k_g, mem_q_g):
    modes = np.zeros((IN_COLS // PROJ_TN,), np.int32)
    gain = [jnp.ones((IN_CONV,), F32)]
    for g in range(SW_GROUPS):
        t0 = (COL_SW + 3 * g * SW_OUT) // PROJ_TN
        modes[t0:t0 + 2 * SW_OUT // PROJ_TN] = TILE_NORM128
        gain += [jnp.tile(sw_q_g[g], SW_HG) * (SW_HD ** -0.5), jnp.tile(sw_k_g[g], SW_HG), jnp.ones((SW_OUT,), F32)]
    modes[COL_MEMQ // PROJ_TN:COL_GATE // PROJ_TN] = TILE_NORM256
    gain += [jnp.tile(mem_q_g, MEM_H) * (MEM_HD ** -0.5), jnp.ones((3 * D_MODEL,), F32)]
    return modes, jnp.concatenate(gain)


def _layer(x, conv_prev, sw_bufs, mem_k, mem_v, w):
    b, t, d = x.shape
    n = b * t
    x2 = x.reshape(n, d)
    hcat = norm_matmul(x2, w["norm1_g"], w["w_in"], *w["in_epilogue"])
    h3 = hcat.reshape(b, t, IN_COLS)
    c_act, conv_state = conv_branch(h3, conv_prev, w["conv_w"], w["conv_b"], w["conv_ln_g"], w["conv_ln_b"])
    sw_o, sw_lse, states = [], [], []
    for g in range(SW_GROUPS):
        if sw_bufs is None:
            o, lse = sw_prompt_group(h3, g)
            keep = min(SW_WINDOWS[g], t)
            for off in (1, 2):
                col = COL_SW + (3 * g + off) * SW_OUT
                states.append(h3[:, t - keep:, col:col + SW_OUT].reshape(b, keep, SW_HG, SW_HD))
        else:
            o, lse, k_new, v_new = sw_sample_group(h3, g, sw_bufs[2 * g], sw_bufs[2 * g + 1])
            states += [k_new, v_new]
        sw_o.append(o.reshape(n, SW_OUT))
        sw_lse.append(lse.reshape(n, SW_OUT))
    if sw_bufs is None:
        mem_o = mem_attention(h3, mem_k, mem_v)
    else:
        mem_o = mem_attention_cache(h3, mem_k, mem_v)
    hres = merge_project(hcat, x2, c_act.reshape(n, C_CONV), sw_o, sw_lse, mem_o.reshape(n, MEM_W),
                         w["w_conv_o"], w["w_sw_o"], w["w_mem_o"], w["w_out"])
    q = norm_matmul(hres, w["norm2_g"], w["peer_w_q"])
    routed = peer_router(q, w["peer_k1"], w["peer_k2"])
    y = peer_experts(hres, w["norm2_g"], routed, w["peer_u"], w["peer_vt"])
    return y.reshape(b, t, d), conv_state, states


def kernel(x_prompt, x_sample, mem_prompt, state_conv, state_sw0_k, state_sw0_v, state_sw1_k, state_sw1_v, state_sw2_k, state_sw2_v, cache_mem_k, cache_mem_v, norm1_g, mem_norm_g, w_in, conv_w, conv_b, conv_ln_g, conv_ln_b, w_conv_o, sw_q_g, sw_k_g, w_sw_o, w_mem_k, w_mem_v, mem_q_g, mem_k_g, w_mem_o, w_out, norm2_g, peer_w_q, peer_k1, peer_k2, peer_u, peer_v):
    depth = w_in.shape[0]
    bp, tp, _ = x_prompt.shape
    n_mem = mem_prompt.shape[1]
    sw_states = (state_sw0_k, state_sw0_v, state_sw1_k, state_sw1_v, state_sw2_k, state_sw2_v)
    yp, ys = x_prompt, x_sample
    conv_p, conv_s, memk_p, memv_p = [], [], [], []
    sw_p = [[] for _ in range(2 * SW_GROUPS)]
    sw_s = [[] for _ in range(2 * SW_GROUPS)]
    for l in range(depth):
        w = dict(
            norm1_g=norm1_g[l], w_in=w_in[l].astype(BF16), conv_w=conv_w[l], conv_b=conv_b[l],
            conv_ln_g=conv_ln_g[l], conv_ln_b=conv_ln_b[l], w_conv_o=w_conv_o[l].astype(BF16),
            w_sw_o=w_sw_o[l].astype(BF16), w_mem_o=w_mem_o[l].astype(BF16), w_out=w_out[l].astype(BF16),
            norm2_g=norm2_g[l], peer_w_q=peer_w_q[l].astype(BF16), peer_k1=peer_k1[l].astype(BF16),
            peer_k2=peer_k2[l].astype(BF16), peer_u=peer_u[l].astype(BF16), peer_vt=peer_v[l].T.astype(BF16),
            in_epilogue=_in_proj_epilogue(sw_q_g[l], sw_k_g[l], mem_q_g[l]))
        w_kv = jnp.concatenate([w_mem_k[l], w_mem_v[l]], axis=1).astype(BF16)
        kv_modes = np.array([TILE_NORM256] * (MEM_W // PROJ_TN) + [TILE_PLAIN] * (MEM_W // PROJ_TN), np.int32)
        kv_gain = jnp.concatenate([jnp.tile(mem_k_g[l], MEM_H), jnp.ones((MEM_W,), F32)])
        kv = norm_matmul(mem_prompt.reshape(bp * n_mem, D_MODEL), mem_norm_g[l], w_kv, kv_modes, kv_gain)
        mk = kv[:, :MEM_W].reshape(bp, n_mem, MEM_W)
        mv = kv[:, MEM_W:].reshape(bp, n_mem, MEM_W)

        zeros_prev = jnp.zeros((bp, CONV_W - 1, C_CONV), F32)
        yp, cp, swp_l = _layer(yp, zeros_prev, None, mk, mv, w)
        ys, cs, sws_l = _layer(ys, state_conv[l], [s[l] for s in sw_states], cache_mem_k[l], cache_mem_v[l], w)
        conv_p.append(cp)
        conv_s.append(cs)
        memk_p.append(mk.reshape(bp, n_mem, MEM_H, MEM_HD))
        memv_p.append(mv.reshape(bp, n_mem, MEM_H, MEM_HD))
        for i in range(2 * SW_GROUPS):
            sw_p[i].append(swp_l[i])
            sw_s[i].append(sws_l[i])
    swp = [jnp.stack(a, axis=0) for a in sw_p]
    sws = [jnp.stack(a, axis=0) for a in sw_s]
    return (yp, ys, jnp.stack(conv_p, axis=0), jnp.stack(conv_s, axis=0),
            swp[0], swp[1], swp[2], swp[3], swp[4], swp[5],
            sws[0], sws[1], sws[2], sws[3], sws[4], sws[5],
            jnp.stack(memk_p, axis=0), jnp.stack(memv_p, axis=0))
```

```python
import functools
import math

import numpy as np
import jax
import jax.numpy as jnp
from jax import lax
from jax.experimental import pallas as pl
from jax.experimental.pallas import tpu as pltpu

F32 = jnp.float32
BF16 = jnp.bfloat16
EPS = 1e-6
NEG_INF = float("-inf")
LOG2_E = 1.0 / math.log(2.0)

LANES = 128
SUBLANES = 8
VMEM_LIMIT = 60 * 1024 * 1024

D_MODEL = 2048
C_CONV = 1024
CONV_W = 31
CONV_PAD = 32
SW_GROUPS = 3
SW_DILATIONS = (1, 4, 16)
SW_WINDOWS = (128, 512, 2048)
SW_SPAN = 128
SW_HG = 4
SW_HD = 128
SW_OUT = SW_HG * SW_HD
SW_CHUNK = 2048
MEM_H = 4
MEM_HD = 256
MEM_W = MEM_H * MEM_HD
IN_CONV = 2 * C_CONV
IN_SW = SW_GROUPS * 3 * SW_OUT
IN_COLS = IN_CONV + IN_SW + MEM_W + 3 * D_MODEL
COL_SW = IN_CONV
COL_MEMQ = IN_CONV + IN_SW
COL_GATE = COL_MEMQ + MEM_W
PROJ_TN = 512
PEER_HEADS = 8
PEER_NKEYS = 128
PEER_DKEY = 256
PEER_TOPK = 16
PEER_TCHUNK = 256

TILE_PLAIN, TILE_NORM128, TILE_NORM256 = 0, 1, 2
_TILE_HD = {TILE_NORM128: SW_HD, TILE_NORM256: MEM_HD}


def _alibi_slopes():
    n = SW_GROUPS * SW_HG
    s = np.float32(2.0) ** (np.float32(-8.0) * np.arange(1, n + 1, dtype=np.float32) / np.float32(n))
    return s.reshape(SW_GROUPS, SW_HG)


def _rms(x, g):
    return x * lax.rsqrt(jnp.mean(x * x, axis=-1, keepdims=True) + EPS) * g


def _dot(a, b):
    return jnp.dot(a, b, preferred_element_type=F32)


def _dot_nt(a, b):
    return lax.dot_general(a, b, (((1,), (1,)), ((), ())), preferred_element_type=F32)


def _params(*sem):
    return pltpu.CompilerParams(dimension_semantics=sem, vmem_limit_bytes=VMEM_LIMIT)


def _pick(n, options):
    for o in options:
        if n % o == 0:
            return o
    raise ValueError(f"no tile in {options} divides {n}")


def _norm_matmul_kernel(mode_ref, x_ref, g_ref, w_ref, cg_ref, o_ref, xn_ref, *, tn, nsub, head_modes):
    j = pl.program_id(1)

    @pl.when(j == 0)
    def _():
        xn_ref[...] = _rms(x_ref[...], g_ref[...]).astype(BF16)

    for k in range(nsub):
        cs = slice(k * tn, (k + 1) * tn)
        acc = _dot(xn_ref[...], w_ref[:, cs])
        if not head_modes:
            o_ref[:, cs] = acc
            continue
        mode = mode_ref[j * nsub + k]

        @pl.when(mode == TILE_PLAIN)
        def _():
            o_ref[:, cs] = acc

        for code in head_modes:
            hd = _TILE_HD[code]

            @pl.when(mode == code)
            def _():
                for h in range(tn // hd):
                    hs = slice(h * hd, (h + 1) * hd)
                    sl = slice(k * tn + h * hd, k * tn + (h + 1) * hd)
                    o_ref[:, sl] = _rms(acc[:, hs], cg_ref[:, sl])


def norm_matmul(x, g, w_bf, tile_modes=None, col_gain=None):
    n, d = x.shape
    cols = w_bf.shape[1]
    tm = _pick(n, (1024, 256))
    tn = PROJ_TN
    nt = cols // tn
    if tile_modes is None:
        tile_modes = np.zeros((nt,), np.int32)
        col_gain = jnp.ones((cols,), F32)
    head_modes = tuple(sorted(set(int(m) for m in tile_modes) - {TILE_PLAIN}))
    nsub = _pick(nt, (3, 2, 1))
    wide = nsub * tn
    grid_spec = pltpu.PrefetchScalarGridSpec(
        num_scalar_prefetch=1,
        grid=(n // tm, nt // nsub),
        in_specs=[pl.BlockSpec((tm, d), lambda i, j, m: (i, 0)),
                  pl.BlockSpec((1, d), lambda i, j, m: (0, 0)),
                  pl.BlockSpec((d, wide), lambda i, j, m: (0, j)),
                  pl.BlockSpec((1, wide), lambda i, j, m: (0, j))],
        out_specs=pl.BlockSpec((tm, wide), lambda i, j, m: (i, j)),
        scratch_shapes=[pltpu.VMEM((tm, d), BF16)])
    return pl.pallas_call(
        functools.partial(_norm_matmul_kernel, tn=tn, nsub=nsub, head_modes=head_modes),
        grid_spec=grid_spec,
        out_shape=jax.ShapeDtypeStruct((n, cols), F32),
        compiler_params=_params("parallel", "arbitrary"),
        name="norm_matmul",
    )(jnp.asarray(tile_modes, jnp.int32), x, g.reshape(1, d), w_bf, col_gain.reshape(1, cols))


def _conv_kernel(hc_ref, prev_ref, cw_ref, cb_ref, lg_ref, lb_ref, act_ref, st_ref, ext_ref, c_ref, *, tt):
    i = pl.program_id(1)
    hist = CONV_W - 1
    lo = CONV_PAD - hist

    @pl.when(i == 0)
    def _():
        ext_ref[lo:CONV_PAD, :] = prev_ref[0]

    @pl.when(i > 0)
    def _():
        ext_ref[lo:CONV_PAD, :] = ext_ref[tt + lo:tt + CONV_PAD, :]

    a = hc_ref[0, :, :C_CONV]
    b = hc_ref[0, :, C_CONV:]
    ext_ref[CONV_PAD:CONV_PAD + tt, :] = a * jax.nn.sigmoid(b)

    for c in range(C_CONV // LANES):
        cs = slice(c * LANES, (c + 1) * LANES)
        acc = jnp.zeros((tt, LANES), F32)
        for w in range(CONV_W):
            acc = acc + ext_ref[lo + w:lo + w + tt, cs] * cw_ref[w:w + 1, cs]
        c_ref[:, cs] = acc + cb_ref[:, cs]

    cv = c_ref[...]
    mu = jnp.mean(cv, axis=-1, keepdims=True)
    xc = cv - mu
    var = jnp.mean(xc * xc, axis=-1, keepdims=True)
    y = xc * lax.rsqrt(var + EPS) * lg_ref[...] + lb_ref[...]
    act_ref[0] = (y * jax.nn.sigmoid(y)).astype(BF16)

    @pl.when(i == pl.num_programs(1) - 1)
    def _():
        st_ref[0] = ext_ref[tt + lo:tt + CONV_PAD, :]


def conv_branch(h3, prev, conv_w, conv_b, ln_g, ln_b):
    b, t, _ = h3.shape
    tt = _pick(t, (128, 8))
    hist = CONV_W - 1
    return pl.pallas_call(
        functools.partial(_conv_kernel, tt=tt),
        grid=(b, t // tt),
        in_specs=[pl.BlockSpec((1, tt, IN_CONV), lambda bi, i: (bi, i, 0)),
                  pl.BlockSpec((1, hist, C_CONV), lambda bi, i: (bi, 0, 0)),
                  pl.BlockSpec((CONV_W, C_CONV), lambda bi, i: (0, 0)),
                  pl.BlockSpec((1, C_CONV), lambda bi, i: (0, 0)),
                  pl.BlockSpec((1, C_CONV), lambda bi, i: (0, 0)),
                  pl.BlockSpec((1, C_CONV), lambda bi, i: (0, 0))],
        out_specs=[pl.BlockSpec((1, tt, C_CONV), lambda bi, i: (bi, i, 0)),
                   pl.BlockSpec((1, hist, C_CONV), lambda bi, i: (bi, 0, 0))],
        out_shape=[jax.ShapeDtypeStruct((b, t, C_CONV), BF16),
                   jax.ShapeDtypeStruct((b, hist, C_CONV), F32)],
        scratch_shapes=[pltpu.VMEM((CONV_PAD + tt, C_CONV), F32),
                        pltpu.VMEM((tt, C_CONV), F32)],
        compiler_params=_params("parallel", "arbitrary"),
        name="conv_branch",
    )(h3, prev, conv_w, conv_b.reshape(1, C_CONV), ln_g.reshape(1, C_CONV), ln_b.reshape(1, C_CONV))


def _with_ones(v):
    return jnp.concatenate([v.astype(BF16), jnp.ones(v.shape, BF16)], axis=-1)


def _softmax_pv(s_a, s_b, v_a, v_b):
    if s_a.shape == s_b.shape:
        m = jnp.max(jnp.maximum(s_a, s_b), axis=-1, keepdims=True)
    else:
        m = jnp.maximum(jnp.max(s_a, axis=-1, keepdims=True), jnp.max(s_b, axis=-1, keepdims=True))
    p_a = jnp.exp(s_a - m).astype(BF16)
    p_b = jnp.exp(s_b - m).astype(BF16)
    hd = v_a.shape[-1]
    ol = _dot(p_a, _with_ones(v_a)) + _dot(p_b, _with_ones(v_b))
    l = ol[:, hd:]
    return ol[:, :hd] / l, m + jnp.log(l)


def _sw_prompt_kernel(slope_ref, q_ref, kc_ref, kp_ref, vc_ref, vp_ref, o_ref, lse_ref, *, r, nq):
    h = pl.program_id(1)
    n = pl.program_id(2)
    blk = SW_SPAN
    slope = slope_ref[h]
    ii = lax.broadcasted_iota(jnp.int32, (blk, blk), 0)
    cc = lax.broadcasted_iota(jnp.int32, (blk, blk), 1)
    back_c = ii - cc
    back_p = back_c + blk
    bias_c = jnp.where(back_c >= 0, -slope * (r * back_c).astype(F32), NEG_INF)
    bias_p = jnp.where(back_p <= SW_SPAN, -slope * (r * back_p).astype(F32), NEG_INF)
    bias_first = bias_p + jnp.where(n >= 1, 0.0, NEG_INF).astype(F32)

    def rows(rho, m):
        start = rho + r * blk * m
        return pl.ds(start, blk) if r == 1 else pl.ds(start, blk, stride=r)

    for rho in range(r):
        for m in range(nq):
            q = q_ref[0, rows(rho, m), :].astype(BF16)
            k_c = kc_ref[0, rows(rho, m), :]
            v_c = vc_ref[0, rows(rho, m), :]
            if m == 0:
                k_p = kp_ref[0, rows(rho, 0), :]
                v_p = vp_ref[0, rows(rho, 0), :]
                b_p = bias_first
            else:
                k_p = kc_ref[0, rows(rho, m - 1), :]
                v_p = vc_ref[0, rows(rho, m - 1), :]
                b_p = bias_p
            s_c = _dot_nt(q, k_c.astype(BF16)) + bias_c
            s_p = _dot_nt(q, k_p.astype(BF16)) + b_p
            o, lse = _softmax_pv(s_c, s_p, v_c, v_p)
            o_ref[0, rows(rho, m), :] = o
            lse_ref[0, rows(rho, m), :] = lse


def sw_prompt_group(h3, g):
    b, t, _ = h3.shape
    r = SW_DILATIONS[g]
    pb = SW_SPAN * r
    nq = max(1, min(t, SW_CHUNK) // pb)
    ch = pb * nq
    assert t % ch == 0
    base = (COL_SW + 3 * g * SW_OUT) // SW_HD
    hpb = SW_OUT // SW_HD
    slopes = jnp.asarray(_alibi_slopes()[g])

    def cur(off):
        return pl.BlockSpec((1, ch, SW_HD), lambda bi, h, n: (bi, n, base + off * hpb + h))

    def prev(off):
        return pl.BlockSpec((1, pb, SW_HD), lambda bi, h, n: (bi, jnp.maximum(n * nq - 1, 0), base + off * hpb + h))

    out_spec = pl.BlockSpec((1, ch, SW_HD), lambda bi, h, n: (bi, n, h))
    out_sds = jax.ShapeDtypeStruct((b, t, SW_OUT), F32)
    return pl.pallas_call(
        functools.partial(_sw_prompt_kernel, r=r, nq=nq),
        grid=(b, SW_HG, t // ch),
        in_specs=[pl.BlockSpec(memory_space=pltpu.SMEM),
                  cur(0), cur(1), prev(1), cur(2), prev(2)],
        out_specs=[out_spec, out_spec],
        out_shape=[out_sds, out_sds],
        compiler_params=_params("parallel", "parallel", "arbitrary"),
        name=f"sw_prompt_g{g}",
    )(slopes, h3, h3, h3, h3, h3)


def _sw_sample_kernel(q_ref, kn_ref, vn_ref, kb_ref, vb_ref, o_ref, lse_ref, ko_ref, vo_ref,
                      *, r, buf_len, t_new, slopes):
    nh = SW_HG
    span = SW_SPAN
    shift = r.bit_length() - 1
    first = buf_len - span * r
    tq = lax.broadcasted_iota(jnp.int32, (t_new, span), 0)
    mm = lax.broadcasted_iota(jnp.int32, (t_new, span), 1)
    back_b = span + (tq >> shift) - mm
    ok_b = back_b <= span
    dist_bf = (r * back_b).astype(F32)
    cls = tq & (r - 1)
    tq2 = lax.broadcasted_iota(jnp.int32, (t_new, t_new), 0)
    tk2 = lax.broadcasted_iota(jnp.int32, (t_new, t_new), 1)
    dist_n = tq2 - tk2
    ok_n = (dist_n >= 0) & ((dist_n & (r - 1)) == 0)
    dist_nf = dist_n.astype(F32)
    keep = (buf_len - t_new) * nh
    ko_ref[pl.ds(0, keep), :] = kb_ref[pl.ds(t_new * nh, keep), :]
    vo_ref[pl.ds(0, keep), :] = vb_ref[pl.ds(t_new * nh, keep), :]
    classes = range(min(r, t_new))
    for h in range(nh):
        sl = slice(h * SW_HD, (h + 1) * SW_HD)
        q = q_ref[0, :, sl].astype(BF16)
        k_n = kn_ref[0, :, sl]
        v_n = vn_ref[0, :, sl]
        ko_ref[pl.ds(keep + h, t_new, stride=nh), :] = k_n
        vo_ref[pl.ds(keep + h, t_new, stride=nh), :] = v_n

        def class_rows(rho):
            return pl.ds((first + rho) * nh + h, span, stride=r * nh)

        s_b = None
        for rho in classes:
            s_c = _dot_nt(q, kb_ref[class_rows(rho), :].astype(BF16))
            s_b = s_c if s_b is None else jnp.where(cls == rho, s_c, s_b)
        s_b = jnp.where(ok_b, s_b - slopes[h] * dist_bf, NEG_INF)
        s_n = jnp.where(ok_n, _dot_nt(q, k_n.astype(BF16)) - slopes[h] * dist_nf, NEG_INF)
        m = jnp.maximum(jnp.max(s_b, axis=-1, keepdims=True), jnp.max(s_n, axis=-1, keepdims=True))
        p_b = jnp.exp(s_b - m)
        ol = _dot(jnp.exp(s_n - m).astype(BF16), _with_ones(v_n))
        for rho in classes:
            p_c = p_b if len(classes) == 1 else jnp.where(cls == rho, p_b, 0.0)
            ol = ol + _dot(p_c.astype(BF16), _with_ones(vb_ref[class_rows(rho), :]))
        l = ol[:, SW_HD:]
        o_ref[0, :, sl] = ol[:, :SW_HD] / l
        lse_ref[0, :, sl] = m + jnp.log(l)


def sw_sample_group(h3, g, k_buf, v_buf):
    b, t, _ = h3.shape
    buf_len = k_buf.shape[1]
    r = SW_DILATIONS[g]
    assert buf_len >= SW_SPAN * r and buf_len % r == 0 and t <= r * SW_SPAN and buf_len > t
    base = COL_SW // SW_OUT + 3 * g
    blk = (1, t, SW_OUT)
    rows = buf_len * SW_HG
    slopes = tuple(float(s) for s in _alibi_slopes()[g])
    out_spec = pl.BlockSpec(blk, lambda bi: (bi, 0, 0))
    buf_spec = pl.BlockSpec((None, rows, SW_HD), lambda bi: (bi, 0, 0))
    out_sds = jax.ShapeDtypeStruct((b, t, SW_OUT), F32)
    buf_sds = jax.ShapeDtypeStruct((b, rows, SW_HD), F32)
    k_buf = k_buf.reshape(b, rows, SW_HD)
    v_buf = v_buf.reshape(b, rows, SW_HD)
    o, lse, k_out, v_out = pl.pallas_call(
        functools.partial(_sw_sample_kernel, r=r, buf_len=buf_len, t_new=t, slopes=slopes),
        grid=(b,),
        in_specs=[pl.BlockSpec(blk, lambda bi: (bi, 0, base)),
                  pl.BlockSpec(blk, lambda bi: (bi, 0, base + 1)),
                  pl.BlockSpec(blk, lambda bi: (bi, 0, base + 2)),
                  buf_spec, buf_spec],
        out_specs=[out_spec, out_spec, buf_spec, buf_spec],
        out_shape=[out_sds, out_sds, buf_sds, buf_sds],
        compiler_params=_params("parallel"),
        name=f"sw_sample_g{g}",
    )(h3, h3, h3, k_buf, v_buf)
    shape4 = (b, buf_len, SW_HG, SW_HD)
    return o, lse, k_out.reshape(shape4), v_out.reshape(shape4)


def _attend(q, k, v):
    s = _dot_nt(q.astype(BF16), k.astype(BF16))
    p = jnp.exp(s - jnp.max(s, axis=-1, keepdims=True)).astype(BF16)
    hd = v.shape[-1]
    ol = _dot(p, _with_ones(v))
    return (ol[:, :hd] / ol[:, hd:]).astype(BF16)


def _mem_attn_kernel(qa_ref, qb_ref, k_ref, v_ref, o_ref):
    per = PROJ_TN // MEM_HD
    for h in range(MEM_H):
        q_ref = (qa_ref, qb_ref)[h // per]
        ql = slice((h % per) * MEM_HD, (h % per + 1) * MEM_HD)
        sl = slice(h * MEM_HD, (h + 1) * MEM_HD)
        o_ref[0, :, sl] = _attend(q_ref[0, :, ql], k_ref[0, :, sl], v_ref[0, :, sl])


def mem_attention(h3, mem_k, mem_v):
    b, t, _ = h3.shape
    n_mem = mem_k.shape[1]
    tq = _pick(t, (512,))
    assert MEM_W == 2 * PROJ_TN
    qbase = COL_MEMQ // PROJ_TN
    kv_spec = pl.BlockSpec((1, n_mem, MEM_W), lambda bi, i: (bi, 0, 0))
    return pl.pallas_call(
        _mem_attn_kernel,
        grid=(b, t // tq),
        in_specs=[pl.BlockSpec((1, tq, PROJ_TN), lambda bi, i: (bi, i, qbase)),
                  pl.BlockSpec((1, tq, PROJ_TN), lambda bi, i: (bi, i, qbase + 1)), kv_spec, kv_spec],
        out_specs=pl.BlockSpec((1, tq, MEM_W), lambda bi, i: (bi, i, 0)),
        out_shape=jax.ShapeDtypeStruct((b, t, MEM_W), BF16),
        compiler_params=_params("parallel", "arbitrary"),
        name="mem_attention",
    )(h3, h3, mem_k, mem_v)


def _mem_attn_cache_kernel(qa_ref, qb_ref, k_ref, v_ref, o_ref):
    per = PROJ_TN // MEM_HD
    for h in range(MEM_H):
        q_ref = (qa_ref, qb_ref)[h // per]
        ql = slice((h % per) * MEM_HD, (h % per + 1) * MEM_HD)
        sl = slice(h * MEM_HD, (h + 1) * MEM_HD)
        o_ref[0, :, sl] = _attend(q_ref[0, :, ql], k_ref[:, h, :], v_ref[:, h, :])


def mem_attention_cache(h3, cache_k, cache_v):
    b, t, _ = h3.shape
    n_mem = cache_k.shape[1]
    assert MEM_W == 2 * PROJ_TN
    qbase = COL_MEMQ // PROJ_TN
    cspec = pl.BlockSpec((None, n_mem, MEM_H, MEM_HD), lambda bi: (bi, 0, 0, 0))
    return pl.pallas_call(
        _mem_attn_cache_kernel,
        grid=(b,),
        in_specs=[pl.BlockSpec((1, t, PROJ_TN), lambda bi: (bi, 0, qbase)),
                  pl.BlockSpec((1, t, PROJ_TN), lambda bi: (bi, 0, qbase + 1)), cspec, cspec],
        out_specs=pl.BlockSpec((1, t, MEM_W), lambda bi: (bi, 0, 0)),
        out_shape=jax.ShapeDtypeStruct((b, t, MEM_W), BF16),
        compiler_params=_params("parallel"),
        name="mem_attention_cache",
    )(h3, h3, cache_k, cache_v)


def _merge_kernel(*refs, n_gate):
    ca_ref, o0_ref, o1_ref, o2_ref, l0_ref, l1_ref, l2_ref, mo_ref = refs[:8]
    gate_refs = refs[8:8 + 3 * n_gate]
    x_ref, wc_ref, ws_ref, wm_ref, wo_ref, h_ref, merged_ref = refs[8 + 3 * n_gate:]
    l0, l1, l2 = l0_ref[...], l1_ref[...], l2_ref[...]
    m = jnp.maximum(jnp.maximum(l0, l1), l2)
    e0, e1, e2 = jnp.exp(l0 - m), jnp.exp(l1 - m), jnp.exp(l2 - m)
    sw = ((e0 * o0_ref[...] + e1 * o1_ref[...] + e2 * o2_ref[...]) / (e0 + e1 + e2)).astype(BF16)
    tn = D_MODEL // n_gate
    for j in range(n_gate):
        cs = slice(j * tn, (j + 1) * tn)
        merged = (jax.nn.sigmoid(gate_refs[j][...]) * _dot(ca_ref[...], wc_ref[:, cs])
                  + jax.nn.sigmoid(gate_refs[n_gate + j][...]) * _dot(sw, ws_ref[:, cs])
                  + jax.nn.sigmoid(gate_refs[2 * n_gate + j][...]) * _dot(mo_ref[...], wm_ref[:, cs]))
        merged_ref[:, cs] = merged.astype(BF16)
    h_ref[...] = x_ref[...] + _dot(merged_ref[...], wo_ref[...])


def merge_project(hcat, x, c_act, sw_o, sw_lse, mem_o, w_conv_o, w_sw_o, w_mem_o, w_out):
    n = hcat.shape[0]
    tm = 256
    tn = PROJ_TN
    n_gate = D_MODEL // tn
    gbase = COL_GATE // tn
    once = pl.Buffered(1)

    def row(width):
        return pl.BlockSpec((tm, width), lambda i: (i, 0))

    def gate(k):
        return pl.BlockSpec((tm, tn), lambda i: (i, gbase + k))

    def weight(w):
        return pl.BlockSpec(w.shape, lambda i: (0, 0), pipeline_mode=once)

    weights = (w_conv_o, w_sw_o, w_mem_o, w_out)
    return pl.pallas_call(
        functools.partial(_merge_kernel, n_gate=n_gate),
        grid=(n // tm,),
        in_specs=[row(C_CONV)] + [row(SW_OUT)] * 6 + [row(MEM_W)] + [gate(k) for k in range(3 * n_gate)]
                 + [row(D_MODEL)] + [weight(w) for w in weights],
        out_specs=row(D_MODEL),
        out_shape=jax.ShapeDtypeStruct((n, D_MODEL), F32),
        scratch_shapes=[pltpu.VMEM((tm, D_MODEL), BF16)],
        compiler_params=_params("parallel"),
        name="merge_project",
    )(c_act, *sw_o, *sw_lse, mem_o, *([hcat] * (3 * n_gate)), x, *weights)


def _sorting_network(n):
    def merge(lo, hi, r):
        step = 2 * r
        if step < hi - lo:
            yield from merge(lo, hi, step)
            yield from merge(lo + r, hi, step)
            yield from ((i, i + r) for i in range(lo + r, hi - r, step))
        else:
            yield (lo, lo + r)

    def sort(lo, hi):
        if hi > lo:
            mid = lo + (hi - lo) // 2
            yield from sort(lo, mid)
            yield from sort(mid + 1, hi)
            yield from merge(lo, hi, 1)

    return tuple(sort(0, n - 1))


def _pop_heads(lists, popped):
    return [jnp.where(popped, lists[d + 1], lists[d]) for d in range(len(lists) - 1)]


def _top_values(s, rows16, k):
    lists = [s[i * SUBLANES:(i + 1) * SUBLANES, :] for i in range(s.shape[0] // SUBLANES)]
    assert len(lists) == k
    for i, j in _sorting_network(len(lists)):
        lists[i], lists[j] = jnp.maximum(lists[i], lists[j]), jnp.minimum(lists[i], lists[j])
    vals = jnp.full(rows16.shape, NEG_INF, F32)
    for i in range(k):
        mx = jnp.max(lists[0], axis=0, keepdims=True)
        vals = jnp.where(rows16 == i, mx, vals)
        lists = _pop_heads(lists, lists[0] == mx)
    return vals


def _pair_sums(v1, v2, rows8, k):
    lo = [jnp.where((rows8 + 1) * (j + 1) <= k, v1[:SUBLANES] + v2[j:j + 1], NEG_INF) for j in range(k)]
    return lo + [v1[SUBLANES:] + v2[0:1]]


def _kth_largest(cands, k):
    lo, hi = list(cands[:-1]), cands[-1]
    tau = None
    for _ in range(k):
        tau = jnp.max(jnp.maximum(lo[0], hi), axis=0, keepdims=True)
        hi = jnp.where(hi == tau, NEG_INF, hi)
        lo = _pop_heads(lo, lo[0] == tau) if len(lo) > 1 else lo
    return tau


def _router_kernel(q_ref, k1_ref, k2_ref, s1_ref, s2_ref, tau_ref, *, tm):
    half = PEER_DKEY // 2
    rows16 = lax.broadcasted_iota(jnp.int32, (PEER_TOPK, tm), 0)
    rows8 = lax.broadcasted_iota(jnp.int32, (SUBLANES, tm), 0)
    k1 = k1_ref[...]
    k2 = k2_ref[...]

    def head(h, carry):
        c0 = pl.multiple_of(h * PEER_DKEY, PEER_DKEY)
        qa = q_ref[:, pl.ds(c0, half)].astype(BF16)
        qb = q_ref[:, pl.ds(c0 + half, half)].astype(BF16)
        s1 = _dot_nt(k1, qa)
        s2 = _dot_nt(k2, qb)
        s1 = s1 - jnp.max(s1, axis=0, keepdims=True)
        s2 = s2 - jnp.max(s2, axis=0, keepdims=True)
        v1 = _top_values(s1, rows16, PEER_TOPK)
        v2 = _top_values(s2, rows16, PEER_TOPK)
        cands = _pair_sums(v1, v2, rows8, PEER_TOPK)
        tau = _kth_largest(cands, PEER_TOPK)
        z = jnp.zeros((SUBLANES, tm), F32)
        for c in cands:
            z = z + jnp.where(c >= tau, jnp.exp(c), 0.0)
        log_z = jnp.log(jnp.sum(z, axis=0, keepdims=True))
        cands_2 = _pair_sums((v1 - log_z) * LOG2_E, v2 * LOG2_E, rows8, PEER_TOPK)
        tau_n = jnp.full((SUBLANES, tm), jnp.inf, F32)
        for c, c2 in zip(cands, cands_2):
            tau_n = jnp.minimum(tau_n, jnp.where(c >= tau, c2, jnp.inf))
        s1_ref[h] = (s1 - log_z) * LOG2_E
        s2_ref[h] = s2 * LOG2_E
        tau_ref[pl.ds(h, 1), :] = jnp.min(tau_n, axis=0, keepdims=True)
        return carry

    lax.fori_loop(0, PEER_HEADS, head, 0)


def peer_router(q, k1_bf, k2_bf):
    n = q.shape[0]
    tm = 256
    big = pl.BlockSpec((PEER_HEADS, PEER_NKEYS, tm), lambda i: (0, 0, i))
    big_sds = jax.ShapeDtypeStruct((PEER_HEADS, PEER_NKEYS, n), F32)
    return pl.pallas_call(
        functools.partial(_router_kernel, tm=tm),
        grid=(n // tm,),
        in_specs=[pl.BlockSpec((tm, PEER_HEADS * PEER_DKEY), lambda i: (i, 0)),
                  pl.BlockSpec((PEER_NKEYS, PEER_DKEY // 2), lambda i: (0, 0)),
                  pl.BlockSpec((PEER_NKEYS, PEER_DKEY // 2), lambda i: (0, 0))],
        out_specs=[big, big, pl.BlockSpec((PEER_HEADS, tm), lambda i: (0, i))],
        out_shape=[big_sds, big_sds, jax.ShapeDtypeStruct((PEER_HEADS, n), F32)],
        compiler_params=_params("parallel"),
        name="peer_router",
    )(q, k1_bf, k2_bf)


def _gelu_tanh(x):
    c = 2.0 * math.sqrt(2.0 / math.pi) * LOG2_E
    return x / (1.0 + jnp.exp2(x * (-c - (0.044715 * c) * (x * x))))


def _peer_kernel(h_ref, g2_ref, s1_ref, s2_ref, tau_ref, u_ref, vt_ref, y_ref,
                 hn_ref, acc_ref, act_new_ref, act_old_ref, coef_new_ref, coef_old_ref, *, tm, te, nj):
    s = pl.program_id(1)
    rows = te // PEER_NKEYS
    pack = 2 * SUBLANES
    chunks = [slice(t, t + PEER_TCHUNK) for t in range(0, tm, PEER_TCHUNK)]

    @pl.when(s == 0)
    def _():
        for ts in chunks:
            hn_ref[:, ts] = _rms(h_ref[ts, :], g2_ref[...]).T.astype(BF16)
        acc_ref[...] = jnp.zeros_like(acc_ref)
        act_old_ref[...] = jnp.zeros_like(act_old_ref)
        coef_old_ref[...] = jnp.zeros_like(coef_old_ref)

    upper = (jnp.clip(s - 1, 0, nj - 1) % 2) == 1

    def row_of(ref, h, al, cs):
        picked = jnp.where(upper, ref[h, rows + al:rows + al + 1, cs], ref[h, al:al + 1, cs])
        return jnp.broadcast_to(picked, (pack, LANES))

    def score(ts):
        act_new_ref[:, ts] = _dot(u_ref[...], hn_ref[:, ts])

    def accumulate(ts):
        acc_ref[:, ts] += _dot(vt_ref[...], coef_old_ref[:, ts])

    def gate_tile(al, c):
        cs = slice(c * LANES, (c + 1) * LANES)
        s1c, tau = [], []
        for h in range(PEER_HEADS):
            s1c.append(row_of(s1_ref, h, al, cs))
            tau.append(jnp.broadcast_to(tau_ref[h:h + 1, cs], (pack, LANES)))
        for sb in range(PEER_NKEYS // pack):
            bs = slice(sb * pack, (sb + 1) * pack)
            gate = jnp.zeros((pack, LANES), F32)
            for h in range(PEER_HEADS):
                total = s1c[h] + s2_ref[h, bs, cs]
                gate = gate + jnp.where(total >= tau[h], jnp.exp2(total), 0.0)
            er = slice(al * PEER_NKEYS + sb * pack, al * PEER_NKEYS + (sb + 1) * pack)
            coef_new_ref[er, cs] = (gate * _gelu_tanh(act_old_ref[er, cs])).astype(BF16)

    matmuls = []
    for ts in chunks:
        matmuls += [functools.partial(score, ts), functools.partial(accumulate, ts)]
    tiles = [(al, c) for al in range(rows) for c in range(tm // LANES)]
    per = -(-len(tiles) // len(matmuls))
    for k, mm in enumerate(matmuls):
        mm()
        for al, c in tiles[k * per:(k + 1) * per]:
            gate_tile(al, c)
    act_old_ref[...] = act_new_ref[...]
    coef_old_ref[...] = coef_new_ref[...]

    @pl.when(s == pl.num_programs(1) - 1)
    def _():
        for ts in chunks:
            y_ref[ts, :] = h_ref[ts, :] + acc_ref[:, ts].T


def peer_experts(h, norm2_g, routed, u_bf, vt_bf):
    n, d = h.shape
    n_exp = u_bf.shape[0]
    tm = _pick(n, (1024, 256))
    te = 512
    nj = n_exp // te
    rows = te // PEER_NKEYS
    assert 2 * rows == SUBLANES
    assert tm % PEER_TCHUNK == 0
    s1, s2, tau = routed

    def blk(off):
        return lambda i, s: jnp.clip(s - off, 0, nj - 1)

    once = pl.Buffered(1)
    row = pl.BlockSpec((PEER_HEADS, SUBLANES, tm), lambda i, s: (0, blk(1)(i, s) * rows // SUBLANES, i))
    big = pl.BlockSpec((PEER_HEADS, PEER_NKEYS, tm), lambda i, s: (0, 0, i), pipeline_mode=once)
    return pl.pallas_call(
        functools.partial(_peer_kernel, tm=tm, te=te, nj=nj),
        grid=(n // tm, nj + 2),
        in_specs=[pl.BlockSpec((tm, d), lambda i, s: (i, 0), pipeline_mode=once),
                  pl.BlockSpec((1, d), lambda i, s: (0, 0)),
                  row, big,
                  pl.BlockSpec((PEER_HEADS, tm), lambda i, s: (0, i)),
                  pl.BlockSpec((te, d), lambda i, s: (blk(0)(i, s), 0)),
                  pl.BlockSpec((d, te), lambda i, s: (0, blk(2)(i, s)))],
        out_specs=pl.BlockSpec((tm, d), lambda i, s: (i, 0), pipeline_mode=once),
        out_shape=jax.ShapeDtypeStruct((n, d), F32),
        scratch_shapes=[pltpu.VMEM((d, tm), BF16),
                        pltpu.VMEM((d, tm), F32),
                        pltpu.VMEM((te, tm), F32), pltpu.VMEM((te, tm), F32),
                        pltpu.VMEM((te, tm), BF16), pltpu.VMEM((te, tm), BF16)],
        compiler_params=_params("parallel", "arbitrary"),
        name="peer_experts",
    )(h, norm2_g.reshape(1, d), s1, s2, tau, u_bf, vt_bf)


def _in_proj_epilogue(sw_q_g, sw_k_g, mem_q_g):
    modes = np.zeros((IN_COLS // PROJ_TN,), np.int32)
    gain = [jnp.ones((IN_CONV,), F32)]
    for g in range(SW_GROUPS):
        t0 = (COL_SW + 3 * g * SW_OUT) // PROJ_TN
        modes[t0:t0 + 2 * SW_OUT // PROJ_TN] = TILE_NORM128
        gain += [jnp.tile(sw_q_g[g], SW_HG) * (SW_HD ** -0.5), jnp.tile(sw_k_g[g], SW_HG), jnp.ones((SW_OUT,), F32)]
    modes[COL_MEMQ // PROJ_TN:COL_GATE // PROJ_TN] = TILE_NORM256
    gain += [jnp.tile(mem_q_g, MEM_H) * (MEM_HD ** -0.5), jnp.ones((3 * D_MODEL,), F32)]
    return modes, jnp.concatenate(gain)


def _layer(x, conv_prev, sw_bufs, mem_k, mem_v, w):
    b, t, d = x.shape
    n = b * t
    x2 = x.reshape(n, d)
    hcat = norm_matmul(x2, w["norm1_g"], w["w_in"], *w["in_epilogue"])
    h3 = hcat.reshape(b, t, IN_COLS)
    c_act, conv_state = conv_branch(h3, conv_prev, w["conv_w"], w["conv_b"], w["conv_ln_g"], w["conv_ln_b"])
    sw_o, sw_lse, states = [], [], []
    for g in range(SW_GROUPS):
        if sw_bufs is None:
            o, lse = sw_prompt_group(h3, g)
            keep = min(SW_WINDOWS[g], t)
            for off in (1, 2):
                col = COL_SW + (3 * g + off) * SW_OUT
                states.append(h3[:, t - keep:, col:col + SW_OUT].reshape(b, keep, SW_HG, SW_HD))
        else:
            o, lse, k_new, v_new = sw_sample_group(h3, g, sw_bufs[2 * g], sw_bufs[2 * g + 1])
            states += [k_new, v_new]
        sw_o.append(o.reshape(n, SW_OUT))
        sw_lse.append(lse.reshape(n, SW_OUT))
    if sw_bufs is None:
        mem_o = mem_attention(h3, mem_k, mem_v)
    else:
        mem_o = mem_attention_cache(h3, mem_k, mem_v)
    hres = merge_project(hcat, x2, c_act.reshape(n, C_CONV), sw_o, sw_lse, mem_o.reshape(n, MEM_W),
                         w["w_conv_o"], w["w_sw_o"], w["w_mem_o"], w["w_out"])
    q = norm_matmul(hres, w["norm2_g"], w["peer_w_q"])
    routed = peer_router(q, w["peer_k1"], w["peer_k2"])
    y = peer_experts(hres, w["norm2_g"], routed, w["peer_u"], w["peer_vt"])
    return y.reshape(b, t, d), conv_state, states


def kernel(x_prompt, x_sample, mem_prompt, state_conv, state_sw0_k, state_sw0_v, state_sw1_k, state_sw1_v, state_sw2_k, state_sw2_v, cache_mem_k, cache_mem_v, norm1_g, mem_norm_g, w_in, conv_w, conv_b, conv_ln_g, conv_ln_b, w_conv_o, sw_q_g, sw_k_g, w_sw_o, w_mem_k, w_mem_v, mem_q_g, mem_k_g, w_mem_o, w_out, norm2_g, peer_w_q, peer_k1, peer_k2, peer_u, peer_v):
    depth = w_in.shape[0]
    bp, tp, _ = x_prompt.shape
    n_mem = mem_prompt.shape[1]
    sw_states = (state_sw0_k, state_sw0_v, state_sw1_k, state_sw1_v, state_sw2_k, state_sw2_v)
    yp, ys = x_prompt, x_sample
    conv_p, conv_s, memk_p, memv_p = [], [], [], []
    sw_p = [[] for _ in range(2 * SW_GROUPS)]
    sw_s = [[] for _ in range(2 * SW_GROUPS)]
    for l in range(depth):
        w = dict(
            norm1_g=norm1_g[l], w_in=w_in[l].astype(BF16), conv_w=conv_w[l], conv_b=conv_b[l],
            conv_ln_g=conv_ln_g[l], conv_ln_b=conv_ln_b[l], w_conv_o=w_conv_o[l].astype(BF16),
            w_sw_o=w_sw_o[l].astype(BF16), w_mem_o=w_mem_o[l].astype(BF16), w_out=w_out[l].astype(BF16),
            norm2_g=norm2_g[l], peer_w_q=peer_w_q[l].astype(BF16), peer_k1=peer_k1[l].astype(BF16),
            peer_k2=peer_k2[l].astype(BF16), peer_u=peer_u[l].astype(BF16), peer_vt=peer_v[l].T.astype(BF16),
            in_epilogue=_in_proj_epilogue(sw_q_g[l], sw_k_g[l], mem_q_g[l]))
        w_kv = jnp.concatenate([w_mem_k[l], w_mem_v[l]], axis=1).astype(BF16)
        kv_modes = np.array([TILE_NORM256] * (MEM_W // PROJ_TN) + [TILE_PLAIN] * (MEM_W // PROJ_TN), np.int32)
        kv_gain = jnp.concatenate([jnp.tile(mem_k_g[l], MEM_H), jnp.ones((MEM_W,), F32)])
        kv = norm_matmul(mem_prompt.reshape(bp * n_mem, D_MODEL), mem_norm_g[l], w_kv, kv_modes, kv_gain)
        mk = kv[:, :MEM_W].reshape(bp, n_mem, MEM_W)
        mv = kv[:, MEM_W:].reshape(bp, n_mem, MEM_W)

        zeros_prev = jnp.zeros((bp, CONV_W - 1, C_CONV), F32)
        yp, cp, swp_l = _layer(yp, zeros_prev, None, mk, mv, w)
        ys, cs, sws_l = _layer(ys, state_conv[l], [s[l] for s in sw_states], cache_mem_k[l], cache_mem_v[l], w)
        conv_p.append(cp)
        conv_s.append(cs)
        memk_p.append(mk.reshape(bp, n_mem, MEM_H, MEM_HD))
        memv_p.append(mv.reshape(bp, n_mem, MEM_H, MEM_HD))
        for i in range(2 * SW_GROUPS):
            sw_p[i].append(swp_l[i])
            sw_s[i].append(sws_l[i])
    swp = [jnp.stack(a, axis=0) for a in sw_p]
    sws = [jnp.stack(a, axis=0) for a in sw_s]
    return (yp, ys, jnp.stack(conv_p, axis=0), jnp.stack(conv_s, axis=0),
            swp[0], swp[1], swp[2], swp[3], swp[4], swp[5],
            sws[0], sws[1], sws[2], sws[3], sws[4], sws[5],
            jnp.stack(memk_p, axis=0), jnp.stack(memv_p, axis=0))
```

```python
import functools
import math

import numpy as np
import jax
import jax.numpy as jnp
from jax import lax
from jax.experimental import pallas as pl
from jax.experimental.pallas import tpu as pltpu

F32 = jnp.float32
BF16 = jnp.bfloat16
EPS = 1e-6
NEG_INF = float("-inf")
LOG2_E = 1.0 / math.log(2.0)

LANES = 128
SUBLANES = 8
VMEM_LIMIT = 60 * 1024 * 1024

D_MODEL = 2048
C_CONV = 1024
CONV_W = 31
CONV_PAD = 32
SW_GROUPS = 3
SW_DILATIONS = (1, 4, 16)
SW_WINDOWS = (128, 512, 2048)
SW_SPAN = 128
SW_HG = 4
SW_HD = 128
SW_OUT = SW_HG * SW_HD
SW_CHUNK = 2048
MEM_H = 4
MEM_HD = 256
MEM_W = MEM_H * MEM_HD
IN_CONV = 2 * C_CONV
IN_SW = SW_GROUPS * 3 * SW_OUT
IN_COLS = IN_CONV + IN_SW + MEM_W + 3 * D_MODEL
COL_SW = IN_CONV
COL_MEMQ = IN_CONV + IN_SW
COL_GATE = COL_MEMQ + MEM_W
PROJ_TN = 512
PEER_HEADS = 8
PEER_NKEYS = 128
PEER_DKEY = 256
PEER_TOPK = 16
PEER_TCHUNK = 256

TILE_PLAIN, TILE_NORM128, TILE_NORM256 = 0, 1, 2
_TILE_HD = {TILE_NORM128: SW_HD, TILE_NORM256: MEM_HD}


def _alibi_slopes():
    n = SW_GROUPS * SW_HG
    s = np.float32(2.0) ** (np.float32(-8.0) * np.arange(1, n + 1, dtype=np.float32) / np.float32(n))
    return s.reshape(SW_GROUPS, SW_HG)


def _rms(x, g):
    return x * lax.rsqrt(jnp.mean(x * x, axis=-1, keepdims=True) + EPS) * g


def _dot(a, b):
    return jnp.dot(a, b, preferred_element_type=F32)


def _dot_nt(a, b):
    return lax.dot_general(a, b, (((1,), (1,)), ((), ())), preferred_element_type=F32)


def _params(*sem):
    return pltpu.CompilerParams(dimension_semantics=sem, vmem_limit_bytes=VMEM_LIMIT)


def _pick(n, options):
    for o in options:
        if n % o == 0:
            return o
    raise ValueError(f"no tile in {options} divides {n}")


def _norm_matmul_kernel(mode_ref, x_ref, g_ref, w_ref, cg_ref, o_ref, xn_ref, *, tn, nsub, head_modes):
    j = pl.program_id(1)

    @pl.when(j == 0)
    def _():
        xn_ref[...] = _rms(x_ref[...], g_ref[...]).astype(BF16)

    for k in range(nsub):
        cs = slice(k * tn, (k + 1) * tn)
        acc = _dot(xn_ref[...], w_ref[:, cs])
        if not head_modes:
            o_ref[:, cs] = acc
            continue
        mode = mode_ref[j * nsub + k]

        @pl.when(mode == TILE_PLAIN)
        def _():
            o_ref[:, cs] = acc

        for code in head_modes:
            hd = _TILE_HD[code]

            @pl.when(mode == code)
            def _():
                for h in range(tn // hd):
                    hs = slice(h * hd, (h + 1) * hd)
                    sl = slice(k * tn + h * hd, k * tn + (h + 1) * hd)
                    o_ref[:, sl] = _rms(acc[:, hs], cg_ref[:, sl])


def norm_matmul(x, g, w_bf, tile_modes=None, col_gain=None):
    n, d = x.shape
    cols = w_bf.shape[1]
    tm = _pick(n, (1024, 256))
    tn = PROJ_TN
    nt = cols // tn
    if tile_modes is None:
        tile_modes = np.zeros((nt,), np.int32)
        col_gain = jnp.ones((cols,), F32)
    head_modes = tuple(sorted(set(int(m) for m in tile_modes) - {TILE_PLAIN}))
    nsub = _pick(nt, (3, 2, 1))
    wide = nsub * tn
    grid_spec = pltpu.PrefetchScalarGridSpec(
        num_scalar_prefetch=1,
        grid=(n // tm, nt // nsub),
        in_specs=[pl.BlockSpec((tm, d), lambda i, j, m: (i, 0)),
                  pl.BlockSpec((1, d), lambda i, j, m: (0, 0)),
                  pl.BlockSpec((d, wide), lambda i, j, m: (0, j)),
                  pl.BlockSpec((1, wide), lambda i, j, m: (0, j))],
        out_specs=pl.BlockSpec((tm, wide), lambda i, j, m: (i, j)),
        scratch_shapes=[pltpu.VMEM((tm, d), BF16)])
    return pl.pallas_call(
        functools.partial(_norm_matmul_kernel, tn=tn, nsub=nsub, head_modes=head_modes),
        grid_spec=grid_spec,
        out_shape=jax.ShapeDtypeStruct((n, cols), F32),
        compiler_params=_params("parallel", "arbitrary"),
        name="norm_matmul",
    )(jnp.asarray(tile_modes, jnp.int32), x, g.reshape(1, d), w_bf, col_gain.reshape(1, cols))


def _conv_kernel(hc_ref, prev_ref, cw_ref, cb_ref, lg_ref, lb_ref, act_ref, st_ref, ext_ref, c_ref, *, tt):
    i = pl.program_id(1)
    hist = CONV_W - 1
    lo = CONV_PAD - hist

    @pl.when(i == 0)
    def _():
        ext_ref[lo:CONV_PAD, :] = prev_ref[0]

    @pl.when(i > 0)
    def _():
        ext_ref[lo:CONV_PAD, :] = ext_ref[tt + lo:tt + CONV_PAD, :]

    a = hc_ref[0, :, :C_CONV]
    b = hc_ref[0, :, C_CONV:]
    ext_ref[CONV_PAD:CONV_PAD + tt, :] = a * jax.nn.sigmoid(b)

    for c in range(C_CONV // LANES):
        cs = slice(c * LANES, (c + 1) * LANES)
        acc = cb_ref[:, cs]
        for s in range(SUBLANES):
            taps = [w for w in range(CONV_W) if (lo + w) % SUBLANES == s]
            if not taps:
                continue
            span = tt + (SUBLANES if s else 0)
            z = None
            for w in taps:
                term = ext_ref[lo + w - s:lo + w - s + span, cs] * cw_ref[w:w + 1, cs]
                z = term if z is None else z + term
            acc = acc + z[s:s + tt]
        c_ref[:, cs] = acc

    cv = c_ref[...]
    mu = jnp.mean(cv, axis=-1, keepdims=True)
    xc = cv - mu
    var = jnp.mean(xc * xc, axis=-1, keepdims=True)
    y = xc * lax.rsqrt(var + EPS) * lg_ref[...] + lb_ref[...]
    act_ref[0] = (y * jax.nn.sigmoid(y)).astype(BF16)

    @pl.when(i == pl.num_programs(1) - 1)
    def _():
        st_ref[0] = ext_ref[tt + lo:tt + CONV_PAD, :]


def conv_branch(h3, prev, conv_w, conv_b, ln_g, ln_b):
    b, t, _ = h3.shape
    tt = _pick(t, (128, 8))
    hist = CONV_W - 1
    return pl.pallas_call(
        functools.partial(_conv_kernel, tt=tt),
        grid=(b, t // tt),
        in_specs=[pl.BlockSpec((1, tt, IN_CONV), lambda bi, i: (bi, i, 0)),
                  pl.BlockSpec((1, hist, C_CONV), lambda bi, i: (bi, 0, 0)),
                  pl.BlockSpec((CONV_W, C_CONV), lambda bi, i: (0, 0)),
                  pl.BlockSpec((1, C_CONV), lambda bi, i: (0, 0)),
                  pl.BlockSpec((1, C_CONV), lambda bi, i: (0, 0)),
                  pl.BlockSpec((1, C_CONV), lambda bi, i: (0, 0))],
        out_specs=[pl.BlockSpec((1, tt, C_CONV), lambda bi, i: (bi, i, 0)),
                   pl.BlockSpec((1, hist, C_CONV), lambda bi, i: (bi, 0, 0))],
        out_shape=[jax.ShapeDtypeStruct((b, t, C_CONV), BF16),
                   jax.ShapeDtypeStruct((b, hist, C_CONV), F32)],
        scratch_shapes=[pltpu.VMEM((CONV_PAD + tt, C_CONV), F32),
                        pltpu.VMEM((tt, C_CONV), F32)],
        compiler_params=_params("parallel", "arbitrary"),
        name="conv_branch",
    )(h3, prev, conv_w, conv_b.reshape(1, C_CONV), ln_g.reshape(1, C_CONV), ln_b.reshape(1, C_CONV))


def _with_ones(v):
    return jnp.concatenate([v.astype(BF16), jnp.ones(v.shape, BF16)], axis=-1)


def _softmax_pv(s_a, s_b, v_a, v_b):
    if s_a.shape == s_b.shape:
        m = jnp.max(jnp.maximum(s_a, s_b), axis=-1, keepdims=True)
    else:
        m = jnp.maximum(jnp.max(s_a, axis=-1, keepdims=True), jnp.max(s_b, axis=-1, keepdims=True))
    p_a = jnp.exp(s_a - m).astype(BF16)
    p_b = jnp.exp(s_b - m).astype(BF16)
    hd = v_a.shape[-1]
    ol = _dot(p_a, _with_ones(v_a)) + _dot(p_b, _with_ones(v_b))
    l = ol[:, hd:]
    return ol[:, :hd] / l, m + jnp.log(l)


def _sw_prompt_kernel(slope_ref, q_ref, kc_ref, kp_ref, vc_ref, vp_ref, o_ref, lse_ref, *, r, nq):
    h = pl.program_id(1)
    n = pl.program_id(2)
    blk = SW_SPAN
    slope = slope_ref[h]
    ii = lax.broadcasted_iota(jnp.int32, (blk, blk), 0)
    cc = lax.broadcasted_iota(jnp.int32, (blk, blk), 1)
    back_c = ii - cc
    back_p = back_c + blk
    bias_c = jnp.where(back_c >= 0, -slope * (r * back_c).astype(F32), NEG_INF)
    bias_p = jnp.where(back_p <= SW_SPAN, -slope * (r * back_p).astype(F32), NEG_INF)
    bias_first = bias_p + jnp.where(n >= 1, 0.0, NEG_INF).astype(F32)

    def rows(rho, m):
        start = rho + r * blk * m
        return pl.ds(start, blk) if r == 1 else pl.ds(start, blk, stride=r)

    for rho in range(r):
        for m in range(nq):
            q = q_ref[0, rows(rho, m), :].astype(BF16)
            k_c = kc_ref[0, rows(rho, m), :]
            v_c = vc_ref[0, rows(rho, m), :]
            if m == 0:
                k_p = kp_ref[0, rows(rho, 0), :]
                v_p = vp_ref[0, rows(rho, 0), :]
                b_p = bias_first
            else:
                k_p = kc_ref[0, rows(rho, m - 1), :]
                v_p = vc_ref[0, rows(rho, m - 1), :]
                b_p = bias_p
            s_c = _dot_nt(q, k_c.astype(BF16)) + bias_c
            s_p = _dot_nt(q, k_p.astype(BF16)) + b_p
            o, lse = _softmax_pv(s_c, s_p, v_c, v_p)
            o_ref[0, rows(rho, m), :] = o
            lse_ref[0, rows(rho, m), :] = lse


def sw_prompt_group(h3, g):
    b, t, _ = h3.shape
    r = SW_DILATIONS[g]
    pb = SW_SPAN * r
    nq = max(1, min(t, SW_CHUNK) // pb)
    ch = pb * nq
    assert t % ch == 0
    base = (COL_SW + 3 * g * SW_OUT) // SW_HD
    hpb = SW_OUT // SW_HD
    slopes = jnp.asarray(_alibi_slopes()[g])

    def cur(off):
        return pl.BlockSpec((1, ch, SW_HD), lambda bi, h, n: (bi, n, base + off * hpb + h))

    def prev(off):
        return pl.BlockSpec((1, pb, SW_HD), lambda bi, h, n: (bi, jnp.maximum(n * nq - 1, 0), base + off * hpb + h))

    out_spec = pl.BlockSpec((1, ch, SW_HD), lambda bi, h, n: (bi, n, h))
    out_sds = jax.ShapeDtypeStruct((b, t, SW_OUT), F32)
    return pl.pallas_call(
        functools.partial(_sw_prompt_kernel, r=r, nq=nq),
        grid=(b, SW_HG, t // ch),
        in_specs=[pl.BlockSpec(memory_space=pltpu.SMEM),
                  cur(0), cur(1), prev(1), cur(2), prev(2)],
        out_specs=[out_spec, out_spec],
        out_shape=[out_sds, out_sds],
        compiler_params=_params("parallel", "parallel", "arbitrary"),
        name=f"sw_prompt_g{g}",
    )(slopes, h3, h3, h3, h3, h3)


def _sw_sample_kernel(q_ref, kn_ref, vn_ref, kb_ref, vb_ref, o_ref, lse_ref, ko_ref, vo_ref,
                      *, r, buf_len, t_new, slopes):
    nh = SW_HG
    span = SW_SPAN
    shift = r.bit_length() - 1
    first = buf_len - span * r
    tq = lax.broadcasted_iota(jnp.int32, (t_new, span), 0)
    mm = lax.broadcasted_iota(jnp.int32, (t_new, span), 1)
    back_b = span + (tq >> shift) - mm
    ok_b = back_b <= span
    dist_bf = (r * back_b).astype(F32)
    cls = tq & (r - 1)
    tq2 = lax.broadcasted_iota(jnp.int32, (t_new, t_new), 0)
    tk2 = lax.broadcasted_iota(jnp.int32, (t_new, t_new), 1)
    dist_n = tq2 - tk2
    ok_n = (dist_n >= 0) & ((dist_n & (r - 1)) == 0)
    dist_nf = dist_n.astype(F32)
    keep = (buf_len - t_new) * nh
    ko_ref[pl.ds(0, keep), :] = kb_ref[pl.ds(t_new * nh, keep), :]
    vo_ref[pl.ds(0, keep), :] = vb_ref[pl.ds(t_new * nh, keep), :]
    classes = range(min(r, t_new))
    for h in range(nh):
        sl = slice(h * SW_HD, (h + 1) * SW_HD)
        q = q_ref[0, :, sl].astype(BF16)
        k_n = kn_ref[0, :, sl]
        v_n = vn_ref[0, :, sl]
        ko_ref[pl.ds(keep + h, t_new, stride=nh), :] = k_n
        vo_ref[pl.ds(keep + h, t_new, stride=nh), :] = v_n

        def class_rows(rho):
            return pl.ds((first + rho) * nh + h, span, stride=r * nh)

        s_b = None
        for rho in classes:
            s_c = _dot_nt(q, kb_ref[class_rows(rho), :].astype(BF16))
            s_b = s_c if s_b is None else jnp.where(cls == rho, s_c, s_b)
        s_b = jnp.where(ok_b, s_b - slopes[h] * dist_bf, NEG_INF)
        s_n = jnp.where(ok_n, _dot_nt(q, k_n.astype(BF16)) - slopes[h] * dist_nf, NEG_INF)
        m = jnp.maximum(jnp.max(s_b, axis=-1, keepdims=True), jnp.max(s_n, axis=-1, keepdims=True))
        p_b = jnp.exp(s_b - m)
        ol = _dot(jnp.exp(s_n - m).astype(BF16), _with_ones(v_n))
        for rho in classes:
            p_c = p_b if len(classes) == 1 else jnp.where(cls == rho, p_b, 0.0)
            ol = ol + _dot(p_c.astype(BF16), _with_ones(vb_ref[class_rows(rho), :]))
        l = ol[:, SW_HD:]
        o_ref[0, :, sl] = ol[:, :SW_HD] / l
        lse_ref[0, :, sl] = m + jnp.log(l)


def sw_sample_group(h3, g, k_buf, v_buf):
    b, t, _ = h3.shape
    buf_len = k_buf.shape[1]
    r = SW_DILATIONS[g]
    assert buf_len >= SW_SPAN * r and buf_len % r == 0 and t <= r * SW_SPAN and buf_len > t
    base = COL_SW // SW_OUT + 3 * g
    blk = (1, t, SW_OUT)
    rows = buf_len * SW_HG
    slopes = tuple(float(s) for s in _alibi_slopes()[g])
    out_spec = pl.BlockSpec(blk, lambda bi: (bi, 0, 0))
    buf_spec = pl.BlockSpec((None, rows, SW_HD), lambda bi: (bi, 0, 0))
    out_sds = jax.ShapeDtypeStruct((b, t, SW_OUT), F32)
    buf_sds = jax.ShapeDtypeStruct((b, rows, SW_HD), F32)
    k_buf = k_buf.reshape(b, rows, SW_HD)
    v_buf = v_buf.reshape(b, rows, SW_HD)
    o, lse, k_out, v_out = pl.pallas_call(
        functools.partial(_sw_sample_kernel, r=r, buf_len=buf_len, t_new=t, slopes=slopes),
        grid=(b,),
        in_specs=[pl.BlockSpec(blk, lambda bi: (bi, 0, base)),
                  pl.BlockSpec(blk, lambda bi: (bi, 0, base + 1)),
                  pl.BlockSpec(blk, lambda bi: (bi, 0, base + 2)),
                  buf_spec, buf_spec],
        out_specs=[out_spec, out_spec, buf_spec, buf_spec],
        out_shape=[out_sds, out_sds, buf_sds, buf_sds],
        compiler_params=_params("parallel"),
        name=f"sw_sample_g{g}",
    )(h3, h3, h3, k_buf, v_buf)
    shape4 = (b, buf_len, SW_HG, SW_HD)
    return o, lse, k_out.reshape(shape4), v_out.reshape(shape4)


def _attend(q, k, v):
    s = _dot_nt(q.astype(BF16), k.astype(BF16))
    p = jnp.exp(s - jnp.max(s, axis=-1, keepdims=True)).astype(BF16)
    hd = v.shape[-1]
    ol = _dot(p, _with_ones(v))
    return (ol[:, :hd] / ol[:, hd:]).astype(BF16)


def _mem_attn_kernel(qa_ref, qb_ref, k_ref, v_ref, o_ref):
    per = PROJ_TN // MEM_HD
    for h in range(MEM_H):
        q_ref = (qa_ref, qb_ref)[h // per]
        ql = slice((h % per) * MEM_HD, (h % per + 1) * MEM_HD)
        sl = slice(h * MEM_HD, (h + 1) * MEM_HD)
        o_ref[0, :, sl] = _attend(q_ref[0, :, ql], k_ref[0, :, sl], v_ref[0, :, sl])


def mem_attention(h3, mem_k, mem_v):
    b, t, _ = h3.shape
    n_mem = mem_k.shape[1]
    tq = _pick(t, (512,))
    assert MEM_W == 2 * PROJ_TN
    qbase = COL_MEMQ // PROJ_TN
    kv_spec = pl.BlockSpec((1, n_mem, MEM_W), lambda bi, i: (bi, 0, 0))
    return pl.pallas_call(
        _mem_attn_kernel,
        grid=(b, t // tq),
        in_specs=[pl.BlockSpec((1, tq, PROJ_TN), lambda bi, i: (bi, i, qbase)),
                  pl.BlockSpec((1, tq, PROJ_TN), lambda bi, i: (bi, i, qbase + 1)), kv_spec, kv_spec],
        out_specs=pl.BlockSpec((1, tq, MEM_W), lambda bi, i: (bi, i, 0)),
        out_shape=jax.ShapeDtypeStruct((b, t, MEM_W), BF16),
        compiler_params=_params("parallel", "arbitrary"),
        name="mem_attention",
    )(h3, h3, mem_k, mem_v)


def _mem_attn_cache_kernel(qa_ref, qb_ref, k_ref, v_ref, o_ref):
    per = PROJ_TN // MEM_HD
    for h in range(MEM_H):
        q_ref = (qa_ref, qb_ref)[h // per]
        ql = slice((h % per) * MEM_HD, (h % per + 1) * MEM_HD)
        sl = slice(h * MEM_HD, (h + 1) * MEM_HD)
        o_ref[0, :, sl] = _attend(q_ref[0, :, ql], k_ref[:, h, :], v_ref[:, h, :])


def mem_attention_cache(h3, cache_k, cache_v):
    b, t, _ = h3.shape
    n_mem = cache_k.shape[1]
    assert MEM_W == 2 * PROJ_TN
    qbase = COL_MEMQ // PROJ_TN
    cspec = pl.BlockSpec((None, n_mem, MEM_H, MEM_HD), lambda bi: (bi, 0, 0, 0))
    return pl.pallas_call(
        _mem_attn_cache_kernel,
        grid=(b,),
        in_specs=[pl.BlockSpec((1, t, PROJ_TN), lambda bi: (bi, 0, qbase)),
                  pl.BlockSpec((1, t, PROJ_TN), lambda bi: (bi, 0, qbase + 1)), cspec, cspec],
        out_specs=pl.BlockSpec((1, t, MEM_W), lambda bi: (bi, 0, 0)),
        out_shape=jax.ShapeDtypeStruct((b, t, MEM_W), BF16),
        compiler_params=_params("parallel"),
        name="mem_attention_cache",
    )(h3, h3, cache_k, cache_v)


def _merge_kernel(*refs, n_gate):
    ca_ref, o0_ref, o1_ref, o2_ref, l0_ref, l1_ref, l2_ref, mo_ref = refs[:8]
    gate_refs = refs[8:8 + 3 * n_gate]
    x_ref, wc_ref, ws_ref, wm_ref, wo_ref, h_ref, merged_ref = refs[8 + 3 * n_gate:]
    l0, l1, l2 = l0_ref[...], l1_ref[...], l2_ref[...]
    m = jnp.maximum(jnp.maximum(l0, l1), l2)
    e0, e1, e2 = jnp.exp(l0 - m), jnp.exp(l1 - m), jnp.exp(l2 - m)
    sw = ((e0 * o0_ref[...] + e1 * o1_ref[...] + e2 * o2_ref[...]) / (e0 + e1 + e2)).astype(BF16)
    tn = D_MODEL // n_gate
    for j in range(n_gate):
        cs = slice(j * tn, (j + 1) * tn)
        merged = (jax.nn.sigmoid(gate_refs[j][...]) * _dot(ca_ref[...], wc_ref[:, cs])
                  + jax.nn.sigmoid(gate_refs[n_gate + j][...]) * _dot(sw, ws_ref[:, cs])
                  + jax.nn.sigmoid(gate_refs[2 * n_gate + j][...]) * _dot(mo_ref[...], wm_ref[:, cs]))
        merged_ref[:, cs] = merged.astype(BF16)
    h_ref[...] = x_ref[...] + _dot(merged_ref[...], wo_ref[...])


def merge_project(hcat, x, c_act, sw_o, sw_lse, mem_o, w_conv_o, w_sw_o, w_mem_o, w_out):
    n = hcat.shape[0]
    tm = 256
    tn = PROJ_TN
    n_gate = D_MODEL // tn
    gbase = COL_GATE // tn
    once = pl.Buffered(1)

    def row(width):
        return pl.BlockSpec((tm, width), lambda i: (i, 0))

    def gate(k):
        return pl.BlockSpec((tm, tn), lambda i: (i, gbase + k))

    def weight(w):
        return pl.BlockSpec(w.shape, lambda i: (0, 0), pipeline_mode=once)

    weights = (w_conv_o, w_sw_o, w_mem_o, w_out)
    return pl.pallas_call(
        functools.partial(_merge_kernel, n_gate=n_gate),
        grid=(n // tm,),
        in_specs=[row(C_CONV)] + [row(SW_OUT)] * 6 + [row(MEM_W)] + [gate(k) for k in range(3 * n_gate)]
                 + [row(D_MODEL)] + [weight(w) for w in weights],
        out_specs=row(D_MODEL),
        out_shape=jax.ShapeDtypeStruct((n, D_MODEL), F32),
        scratch_shapes=[pltpu.VMEM((tm, D_MODEL), BF16)],
        compiler_params=_params("parallel"),
        name="merge_project",
    )(c_act, *sw_o, *sw_lse, mem_o, *([hcat] * (3 * n_gate)), x, *weights)


def _sorting_network(n):
    def merge(lo, hi, r):
        step = 2 * r
        if step < hi - lo:
            yield from merge(lo, hi, step)
            yield from merge(lo + r, hi, step)
            yield from ((i, i + r) for i in range(lo + r, hi - r, step))
        else:
            yield (lo, lo + r)

    def sort(lo, hi):
        if hi > lo:
            mid = lo + (hi - lo) // 2
            yield from sort(lo, mid)
            yield from sort(mid + 1, hi)
            yield from merge(lo, hi, 1)

    return tuple(sort(0, n - 1))


def _pop_heads(lists, popped):
    return [jnp.where(popped, lists[d + 1], lists[d]) for d in range(len(lists) - 1)]


def _top_values(s, rows16, k):
    lists = [s[i * SUBLANES:(i + 1) * SUBLANES, :] for i in range(s.shape[0] // SUBLANES)]
    assert len(lists) == k
    for i, j in _sorting_network(len(lists)):
        lists[i], lists[j] = jnp.maximum(lists[i], lists[j]), jnp.minimum(lists[i], lists[j])
    vals = jnp.full(rows16.shape, NEG_INF, F32)
    for i in range(k):
        mx = jnp.max(lists[0], axis=0, keepdims=True)
        vals = jnp.where(rows16 == i, mx, vals)
        lists = _pop_heads(lists, lists[0] == mx)
    return vals


def _pair_sums(v1, v2, rows8, k):
    lo = [jnp.where((rows8 + 1) * (j + 1) <= k, v1[:SUBLANES] + v2[j:j + 1], NEG_INF) for j in range(k)]
    return lo + [v1[SUBLANES:] + v2[0:1]]


def _kth_largest(cands, k):
    lo, hi = list(cands[:-1]), cands[-1]
    tau = None
    for _ in range(k):
        tau = jnp.max(jnp.maximum(lo[0], hi), axis=0, keepdims=True)
        hi = jnp.where(hi == tau, NEG_INF, hi)
        lo = _pop_heads(lo, lo[0] == tau) if len(lo) > 1 else lo
    return tau


def _router_kernel(q_ref, k1_ref, k2_ref, s1_ref, s2_ref, tau_ref, *, tm):
    half = PEER_DKEY // 2
    rows16 = lax.broadcasted_iota(jnp.int32, (PEER_TOPK, tm), 0)
    rows8 = lax.broadcasted_iota(jnp.int32, (SUBLANES, tm), 0)
    k1 = k1_ref[...]
    k2 = k2_ref[...]

    def head(h, carry):
        c0 = pl.multiple_of(h * PEER_DKEY, PEER_DKEY)
        qa = q_ref[:, pl.ds(c0, half)].astype(BF16)
        qb = q_ref[:, pl.ds(c0 + half, half)].astype(BF16)
        s1 = _dot_nt(k1, qa)
        s2 = _dot_nt(k2, qb)
        s1 = s1 - jnp.max(s1, axis=0, keepdims=True)
        s2 = s2 - jnp.max(s2, axis=0, keepdims=True)
        v1 = _top_values(s1, rows16, PEER_TOPK)
        v2 = _top_values(s2, rows16, PEER_TOPK)
        cands = _pair_sums(v1, v2, rows8, PEER_TOPK)
        tau = _kth_largest(cands, PEER_TOPK)
        z = jnp.zeros((SUBLANES, tm), F32)
        for c in cands:
            z = z + jnp.where(c >= tau, jnp.exp(c), 0.0)
        log_z = jnp.log(jnp.sum(z, axis=0, keepdims=True))
        cands_2 = _pair_sums((v1 - log_z) * LOG2_E, v2 * LOG2_E, rows8, PEER_TOPK)
        tau_n = jnp.full((SUBLANES, tm), jnp.inf, F32)
        for c, c2 in zip(cands, cands_2):
            tau_n = jnp.minimum(tau_n, jnp.where(c >= tau, c2, jnp.inf))
        s1_ref[h] = (s1 - log_z) * LOG2_E
        s2_ref[h] = s2 * LOG2_E
        tau_ref[pl.ds(h, 1), :] = jnp.min(tau_n, axis=0, keepdims=True)
        return carry

    lax.fori_loop(0, PEER_HEADS, head, 0)


def peer_router(q, k1_bf, k2_bf):
    n = q.shape[0]
    tm = 256
    big = pl.BlockSpec((PEER_HEADS, PEER_NKEYS, tm), lambda i: (0, 0, i))
    big_sds = jax.ShapeDtypeStruct((PEER_HEADS, PEER_NKEYS, n), F32)
    return pl.pallas_call(
        functools.partial(_router_kernel, tm=tm),
        grid=(n // tm,),
        in_specs=[pl.BlockSpec((tm, PEER_HEADS * PEER_DKEY), lambda i: (i, 0)),
                  pl.BlockSpec((PEER_NKEYS, PEER_DKEY // 2), lambda i: (0, 0)),
                  pl.BlockSpec((PEER_NKEYS, PEER_DKEY // 2), lambda i: (0, 0))],
        out_specs=[big, big, pl.BlockSpec((PEER_HEADS, tm), lambda i: (0, i))],
        out_shape=[big_sds, big_sds, jax.ShapeDtypeStruct((PEER_HEADS, n), F32)],
        compiler_params=_params("parallel"),
        name="peer_router",
    )(q, k1_bf, k2_bf)


def _gelu_tanh(x):
    c = 2.0 * math.sqrt(2.0 / math.pi) * LOG2_E
    return x / (1.0 + jnp.exp2(x * (-c - (0.044715 * c) * (x * x))))


def _peer_kernel(h_ref, g2_ref, s1_ref, s2_ref, tau_ref, u_ref, vt_ref, y_ref,
                 hn_ref, acc_ref, act_new_ref, act_old_ref, coef_new_ref, coef_old_ref, *, tm, te, nj):
    s = pl.program_id(1)
    rows = te // PEER_NKEYS
    pack = 2 * SUBLANES
    chunks = [slice(t, t + PEER_TCHUNK) for t in range(0, tm, PEER_TCHUNK)]

    @pl.when(s == 0)
    def _():
        for ts in chunks:
            hn_ref[:, ts] = _rms(h_ref[ts, :], g2_ref[...]).T.astype(BF16)
        acc_ref[...] = jnp.zeros_like(acc_ref)
        act_old_ref[...] = jnp.zeros_like(act_old_ref)
        coef_old_ref[...] = jnp.zeros_like(coef_old_ref)

    upper = (jnp.clip(s - 1, 0, nj - 1) % 2) == 1

    def row_of(ref, h, al, cs):
        picked = jnp.where(upper, ref[h, rows + al:rows + al + 1, cs], ref[h, al:al + 1, cs])
        return jnp.broadcast_to(picked, (pack, LANES))

    def score(ts):
        act_new_ref[:, ts] = _dot(u_ref[...], hn_ref[:, ts])

    def accumulate(ts):
        acc_ref[:, ts] += _dot(vt_ref[...], coef_old_ref[:, ts])

    def gate_tile(al, c):
        cs = slice(c * LANES, (c + 1) * LANES)
        s1c, tau = [], []
        for h in range(PEER_HEADS):
            s1c.append(row_of(s1_ref, h, al, cs))
            tau.append(jnp.broadcast_to(tau_ref[h:h + 1, cs], (pack, LANES)))
        for sb in range(PEER_NKEYS // pack):
            bs = slice(sb * pack, (sb + 1) * pack)
            gate = jnp.zeros((pack, LANES), F32)
            for h in range(PEER_HEADS):
                total = s1c[h] + s2_ref[h, bs, cs]
                gate = gate + jnp.where(total >= tau[h], jnp.exp2(total), 0.0)
            er = slice(al * PEER_NKEYS + sb * pack, al * PEER_NKEYS + (sb + 1) * pack)
            coef_new_ref[er, cs] = (gate * _gelu_tanh(act_old_ref[er, cs])).astype(BF16)

    matmuls = []
    for ts in chunks:
        matmuls += [functools.partial(score, ts), functools.partial(accumulate, ts)]
    tiles = [(al, c) for al in range(rows) for c in range(tm // LANES)]
    per = -(-len(tiles) // len(matmuls))
    for k, mm in enumerate(matmuls):
        mm()
        for al, c in tiles[k * per:(k + 1) * per]:
            gate_tile(al, c)
    act_old_ref[...] = act_new_ref[...]
    coef_old_ref[...] = coef_new_ref[...]

    @pl.when(s == pl.num_programs(1) - 1)
    def _():
        for ts in chunks:
            y_ref[ts, :] = h_ref[ts, :] + acc_ref[:, ts].T


def peer_experts(h, norm2_g, routed, u_bf, vt_bf):
    n, d = h.shape
    n_exp = u_bf.shape[0]
    tm = _pick(n, (1024, 256))
    te = 512
    nj = n_exp // te
    rows = te // PEER_NKEYS
    assert 2 * rows == SUBLANES
    assert tm % PEER_TCHUNK == 0
    s1, s2, tau = routed

    def blk(off):
        return lambda i, s: jnp.clip(s - off, 0, nj - 1)

    once = pl.Buffered(1)
    row = pl.BlockSpec((PEER_HEADS, SUBLANES, tm), lambda i, s: (0, blk(1)(i, s) * rows // SUBLANES, i))
    big = pl.BlockSpec((PEER_HEADS, PEER_NKEYS, tm), lambda i, s: (0, 0, i), pipeline_mode=once)
    return pl.pallas_call(
        functools.partial(_peer_kernel, tm=tm, te=te, nj=nj),
        grid=(n // tm, nj + 2),
        in_specs=[pl.BlockSpec((tm, d), lambda i, s: (i, 0), pipeline_mode=once),
                  pl.BlockSpec((1, d), lambda i, s: (0, 0)),
                  row, big,
                  pl.BlockSpec((PEER_HEADS, tm), lambda i, s: (0, i)),
                  pl.BlockSpec((te, d), lambda i, s: (blk(0)(i, s), 0)),
                  pl.BlockSpec((d, te), lambda i, s: (0, blk(2)(i, s)))],
        out_specs=pl.BlockSpec((tm, d), lambda i, s: (i, 0), pipeline_mode=once),
        out_shape=jax.ShapeDtypeStruct((n, d), F32),
        scratch_shapes=[pltpu.VMEM((d, tm), BF16),
                        pltpu.VMEM((d, tm), F32),
                        pltpu.VMEM((te, tm), F32), pltpu.VMEM((te, tm), F32),
                        pltpu.VMEM((te, tm), BF16), pltpu.VMEM((te, tm), BF16)],
        compiler_params=_params("parallel", "arbitrary"),
        name="peer_experts",
    )(h, norm2_g.reshape(1, d), s1, s2, tau, u_bf, vt_bf)


def _transpose_cast_kernel(x_ref, o_ref, *, tc):
    for c in range(x_ref.shape[1] // tc):
        o_ref[c * tc:(c + 1) * tc, :] = x_ref[:, c * tc:(c + 1) * tc].T.astype(BF16)


def transpose_cast(x):
    rows, cols = x.shape
    tr = 512
    return pl.pallas_call(
        functools.partial(_transpose_cast_kernel, tc=512),
        grid=(rows // tr,),
        in_specs=[pl.BlockSpec((tr, cols), lambda i: (i, 0))],
        out_specs=pl.BlockSpec((cols, tr), lambda i: (0, i)),
        out_shape=jax.ShapeDtypeStruct((cols, rows), BF16),
        compiler_params=_params("parallel"),
        name="transpose_cast",
    )(x)


def _in_proj_epilogue(sw_q_g, sw_k_g, mem_q_g):
    modes = np.zeros((IN_COLS // PROJ_TN,), np.int32)
    gain = [jnp.ones((IN_CONV,), F32)]
    for g in range(SW_GROUPS):
        t0 = (COL_SW + 3 * g * SW_OUT) // PROJ_TN
        modes[t0:t0 + 2 * SW_OUT // PROJ_TN] = TILE_NORM128
        gain += [jnp.tile(sw_q_g[g], SW_HG) * (SW_HD ** -0.5), jnp.tile(sw_k_g[g], SW_HG), jnp.ones((SW_OUT,), F32)]
    modes[COL_MEMQ // PROJ_TN:COL_GATE // PROJ_TN] = TILE_NORM256
    gain += [jnp.tile(mem_q_g, MEM_H) * (MEM_HD ** -0.5), jnp.ones((3 * D_MODEL,), F32)]
    return modes, jnp.concatenate(gain)


def _layer(x, conv_prev, sw_bufs, mem_k, mem_v, w):
    b, t, d = x.shape
    n = b * t
    x2 = x.reshape(n, d)
    hcat = norm_matmul(x2, w["norm1_g"], w["w_in"], *w["in_epilogue"])
    h3 = hcat.reshape(b, t, IN_COLS)
    c_act, conv_state = conv_branch(h3, conv_prev, w["conv_w"], w["conv_b"], w["conv_ln_g"], w["conv_ln_b"])
    sw_o, sw_lse, states = [], [], []
    for g in range(SW_GROUPS):
        if sw_bufs is None:
            o, lse = sw_prompt_group(h3, g)
            keep = min(SW_WINDOWS[g], t)
            for off in (1, 2):
                col = COL_SW + (3 * g + off) * SW_OUT
                states.append(h3[:, t - keep:, col:col + SW_OUT].reshape(b, keep, SW_HG, SW_HD))
        else:
            o, lse, k_new, v_new = sw_sample_group(h3, g, sw_bufs[2 * g], sw_bufs[2 * g + 1])
            states += [k_new, v_new]
        sw_o.append(o.reshape(n, SW_OUT))
        sw_lse.append(lse.reshape(n, SW_OUT))
    if sw_bufs is None:
        mem_o = mem_attention(h3, mem_k, mem_v)
    else:
        mem_o = mem_attention_cache(h3, mem_k, mem_v)
    hres = merge_project(hcat, x2, c_act.reshape(n, C_CONV), sw_o, sw_lse, mem_o.reshape(n, MEM_W),
                         w["w_conv_o"], w["w_sw_o"], w["w_mem_o"], w["w_out"])
    q = norm_matmul(hres, w["norm2_g"], w["peer_w_q"])
    routed = peer_router(q, w["peer_k1"], w["peer_k2"])
    y = peer_experts(hres, w["norm2_g"], routed, w["peer_u"], w["peer_vt"])
    return y.reshape(b, t, d), conv_state, states


def kernel(x_prompt, x_sample, mem_prompt, state_conv, state_sw0_k, state_sw0_v, state_sw1_k, state_sw1_v, state_sw2_k, state_sw2_v, cache_mem_k, cache_mem_v, norm1_g, mem_norm_g, w_in, conv_w, conv_b, conv_ln_g, conv_ln_b, w_conv_o, sw_q_g, sw_k_g, w_sw_o, w_mem_k, w_mem_v, mem_q_g, mem_k_g, w_mem_o, w_out, norm2_g, peer_w_q, peer_k1, peer_k2, peer_u, peer_v):
    depth = w_in.shape[0]
    bp, tp, _ = x_prompt.shape
    n_mem = mem_prompt.shape[1]
    sw_states = (state_sw0_k, state_sw0_v, state_sw1_k, state_sw1_v, state_sw2_k, state_sw2_v)
    yp, ys = x_prompt, x_sample
    conv_p, conv_s, memk_p, memv_p = [], [], [], []
    sw_p = [[] for _ in range(2 * SW_GROUPS)]
    sw_s = [[] for _ in range(2 * SW_GROUPS)]
    for l in range(depth):
        w = dict(
            norm1_g=norm1_g[l], w_in=w_in[l].astype(BF16), conv_w=conv_w[l], conv_b=conv_b[l],
            conv_ln_g=conv_ln_g[l], conv_ln_b=conv_ln_b[l], w_conv_o=w_conv_o[l].astype(BF16),
            w_sw_o=w_sw_o[l].astype(BF16), w_mem_o=w_mem_o[l].astype(BF16), w_out=w_out[l].astype(BF16),
            norm2_g=norm2_g[l], peer_w_q=peer_w_q[l].astype(BF16), peer_k1=peer_k1[l].astype(BF16),
            peer_k2=peer_k2[l].astype(BF16), peer_u=peer_u[l].astype(BF16), peer_vt=transpose_cast(peer_v[l]),
            in_epilogue=_in_proj_epilogue(sw_q_g[l], sw_k_g[l], mem_q_g[l]))
        w_kv = jnp.concatenate([w_mem_k[l], w_mem_v[l]], axis=1).astype(BF16)
        kv_modes = np.array([TILE_NORM256] * (MEM_W // PROJ_TN) + [TILE_PLAIN] * (MEM_W // PROJ_TN), np.int32)
        kv_gain = jnp.concatenate([jnp.tile(mem_k_g[l], MEM_H), jnp.ones((MEM_W,), F32)])
        kv = norm_matmul(mem_prompt.reshape(bp * n_mem, D_MODEL), mem_norm_g[l], w_kv, kv_modes, kv_gain)
        mk = kv[:, :MEM_W].reshape(bp, n_mem, MEM_W)
        mv = kv[:, MEM_W:].reshape(bp, n_mem, MEM_W)

        zeros_prev = jnp.zeros((bp, CONV_W - 1, C_CONV), F32)
        yp, cp, swp_l = _layer(yp, zeros_prev, None, mk, mv, w)
        ys, cs, sws_l = _layer(ys, state_conv[l], [s[l] for s in sw_states], cache_mem_k[l], cache_mem_v[l], w)
        conv_p.append(cp)
        conv_s.append(cs)
        memk_p.append(mk.reshape(bp, n_mem, MEM_H, MEM_HD))
        memv_p.append(mv.reshape(bp, n_mem, MEM_H, MEM_HD))
        for i in range(2 * SW_GROUPS):
            sw_p[i].append(swp_l[i])
            sw_s[i].append(sws_l[i])
    swp = [jnp.stack(a, axis=0) for a in sw_p]
    sws = [jnp.stack(a, axis=0) for a in sw_s]
    return (yp, ys, jnp.stack(conv_p, axis=0), jnp.stack(conv_s, axis=0),
            swp[0], swp[1], swp[2], swp[3], swp[4], swp[5],
            sws[0], sws[1], sws[2], sws[3], sws[4], sws[5],
            jnp.stack(memk_p, axis=0), jnp.stack(memv_p, axis=0))
```
